```python
import math
import jax
import jax.numpy as jnp
from jax import lax
import numpy as np

D_MODEL = 2048
BATCH = 1
SEQ = 8192
DEPTH = 4

D_FF = 5632
CONV_K = 4
SSD_HEADS = 32
SSD_HEAD_DIM = 64
SSD_D_INNER = SSD_HEADS * SSD_HEAD_DIM
SSD_GROUPS = 4
SSD_STATE = 128
SSD_CHUNK = 128
GDN_HEADS = 16
GDN_DK = 128
GDN_DV = 128
GDN_CHUNK = 64
ATTN_HEADS = 16
ATTN_HEAD_DIM = 128
MOBA_BLOCK = 256
MOBA_TOPK = 3
MOBA_Q_BLOCK = 32
REL_BUCKETS = 32
REL_MAX_DIST = 4096
N_BRANCHES = 3
N_MOD = 9
EPS = 1e-6
NEG_INF = -1e30

SSD_XBC = SSD_D_INNER + 2 * SSD_GROUPS * SSD_STATE
GDN_QKV = GDN_HEADS * (2 * GDN_DK + GDN_DV)
ATTN_QKV = 3 * ATTN_HEADS * ATTN_HEAD_DIM
IN_SPLITS = (SSD_D_INNER, SSD_XBC, SSD_HEADS, GDN_QKV, GDN_HEADS * GDN_DV, GDN_HEADS, GDN_HEADS, ATTN_QKV, N_BRANCHES * D_MODEL)
IN_TOTAL = sum(IN_SPLITS)

kernel_name = 'hybrid_ssd_deltanet_moba_macaron'


def _split(u, sizes):
    cuts = [int(s) for s in np.cumsum(sizes)[:-1]]
    return jnp.split(u, cuts, axis=-1)


def _rms(u):
    uf = u.astype(jnp.float32)
    return (uf * lax.rsqrt(jnp.mean(uf * uf, axis=-1, keepdims=True) + EPS)).astype(u.dtype)


def rms_norm(u, g):
    return _rms(u) * g


def l2_norm(u):
    uf = u.astype(jnp.float32)
    return (uf * lax.rsqrt(jnp.sum(uf * uf, axis=-1, keepdims=True) + EPS)).astype(u.dtype)


def modulate(u, g, shift, scale):
    return rms_norm(u, g) * (1.0 + scale) + shift


def swiglu(u, w_gate, w_up, w_down):
    return (jax.nn.silu(u @ w_gate) * (u @ w_up)) @ w_down


def causal_dwconv(u, w):
    k, ch = w.shape
    return lax.conv_general_dilated(u, w[:, None, :], window_strides=(1,), padding=((k - 1, 0),),
                                    dimension_numbers=('NWC', 'WIO', 'NWC'), feature_group_count=ch)


def t5_bucket(dist):
    n = jnp.maximum(dist, 0)
    exact = REL_BUCKETS // 2
    nf = jnp.maximum(n, 1).astype(jnp.float32)
    large = exact + (jnp.log(nf / exact) / math.log(REL_MAX_DIST / exact) * (REL_BUCKETS - exact)).astype(jnp.int32)
    return jnp.where(n < exact, n, jnp.minimum(large, REL_BUCKETS - 1))


def ssd_chunked(x, dt, a, bm, cm):
    bsz, t, nh, hp = x.shape
    ng, ns = bm.shape[2], bm.shape[3]
    nr = nh // ng
    L = SSD_CHUNK
    nc = t // L
    xdt = (x * dt[..., None].astype(x.dtype)).reshape(bsz, nc, L, ng, nr, hp)
    bc = bm.reshape(bsz, nc, L, ng, ns)
    cc = cm.reshape(bsz, nc, L, ng, ns)
    da = (dt * a).reshape(bsz, nc, L, ng, nr).transpose(0, 1, 3, 4, 2)
    acs = jnp.cumsum(da, axis=-1)
    idx = jnp.arange(L)
    tri = idx[:, None] >= idx[None, :]
    lmat = jnp.exp(jnp.where(tri, acs[..., :, None] - acs[..., None, :], -jnp.inf)).astype(x.dtype)
    cb = jnp.einsum('bclgn,bcsgn->bcgls', cc, bc)
    y_diag = jnp.einsum('bcgrls,bcsgrp->bclgrp', lmat * cb[:, :, :, None], xdt)
    decay_states = jnp.exp(acs[..., -1:] - acs).astype(x.dtype)
    states = jnp.einsum('bclgn,bcgrl,bclgrp->bcgrpn', bc, decay_states, xdt)
    chunk_decay = jnp.exp(acs[..., -1]).astype(x.dtype)

    def step(s, inp):
        st, dec = inp
        return s * dec[..., None, None] + st, s

    s0 = jnp.zeros((bsz, ng, nr, hp, ns), x.dtype)
    _, states_in = lax.scan(step, s0, (jnp.moveaxis(states, 1, 0), jnp.moveaxis(chunk_decay, 1, 0)))
    states_in = jnp.moveaxis(states_in, 0, 1)
    y_off = jnp.einsum('bclgn,bcgrpn,bcgrl->bclgrp', cc, states_in, jnp.exp(acs).astype(x.dtype))
    return (y_diag + y_off).reshape(bsz, t, nh, hp)


def ssd_mixer(z, xbc, dt, conv_w, conv_b, dt_bias, a_log, d_skip, norm_g):
    bsz, t, _ = z.shape
    xbc = jax.nn.silu(causal_dwconv(xbc, conv_w) + conv_b)
    xs, bm, cm = _split(xbc, (SSD_D_INNER, SSD_GROUPS * SSD_STATE, SSD_GROUPS * SSD_STATE))
    dt = jax.nn.softplus((dt + dt_bias).astype(jnp.float32))
    a = -jnp.exp(a_log.astype(jnp.float32))
    xh = xs.reshape(bsz, t, SSD_HEADS, SSD_HEAD_DIM)
    y = ssd_chunked(xh, dt, a, bm.reshape(bsz, t, SSD_GROUPS, SSD_STATE), cm.reshape(bsz, t, SSD_GROUPS, SSD_STATE))
    y = y + xh * d_skip[:, None]
    y = y.reshape(bsz, t, SSD_D_INNER) * jax.nn.silu(z)
    y = _rms(y.reshape(bsz, t, SSD_GROUPS, SSD_D_INNER // SSD_GROUPS)).reshape(bsz, t, SSD_D_INNER)
    return y * norm_g


def gated_delta_rule_chunked(q, k, v, g, beta):
    bsz, t, nh, dk = q.shape
    dv = v.shape[-1]
    L = GDN_CHUNK
    nc = t // L

    def chunks(u):
        return jnp.moveaxis(u.reshape((bsz, nc, L, nh) + u.shape[3:]), 3, 1)

    qc, kc, vc, bc = chunks(q), chunks(k), chunks(v), chunks(beta)
    gc = jnp.cumsum(chunks(g).astype(jnp.float32), axis=-1)
    idx = jnp.arange(L)
    incl = idx[:, None] >= idx[None, :]
    strict = idx[:, None] > idx[None, :]
    decay = jnp.exp(jnp.where(incl, gc[..., :, None] - gc[..., None, :], -jnp.inf)).astype(q.dtype)
    kb = kc * bc[..., None]
    a_mat = jnp.where(strict, jnp.einsum('bhcid,bhcjd->bhcij', kb, kc) * decay, 0.0)
    t_mat = (a_mat + jnp.eye(L, dtype=a_mat.dtype)).astype(jnp.float32)
    rhs = jnp.concatenate([vc * bc[..., None], kb * jnp.exp(gc)[..., None].astype(q.dtype)], axis=-1).astype(jnp.float32)
    sol = lax.linalg.triangular_solve(t_mat, rhs, left_side=True, lower=True, unit_diagonal=True).astype(q.dtype)
    u_new, w = sol[..., :dv], sol[..., dv:]
    qk = jnp.where(incl, jnp.einsum('bhcid,bhcjd->bhcij', qc, kc) * decay, 0.0)
    q_dec = qc * jnp.exp(gc)[..., None].astype(q.dtype)
    k_dec = kc * jnp.exp(gc[..., -1:] - gc)[..., None].astype(q.dtype)
    g_last = jnp.exp(gc[..., -1]).astype(q.dtype)

    def step(s, inp):
        u_c, w_c, q_c, k_c, qk_c, gl_c = inp
        v_c = u_c - jnp.einsum('bhld,bhdv->bhlv', w_c, s)
        o = jnp.einsum('bhld,bhdv->bhlv', q_c, s) + jnp.einsum('bhls,bhsv->bhlv', qk_c, v_c)
        s = s * gl_c[..., None, None] + jnp.einsum('bhld,bhlv->bhdv', k_c, v_c)
        return s, o

    xs = tuple(jnp.moveaxis(arr, 2, 0) for arr in (u_new, w, q_dec, k_dec, qk, g_last))
    s0 = jnp.zeros((bsz, nh, dk, dv), q.dtype)
    _, o = lax.scan(step, s0, xs)
    o = jnp.moveaxis(o, 0, 2).reshape(bsz, nh, t, dv)
    return jnp.moveaxis(o, 1, 2)


def gdn_mixer(qkv, z, a, b, conv_w, dt_bias, a_log, norm_g):
    bsz, t, _ = qkv.shape
    qkv = jax.nn.silu(causal_dwconv(qkv, conv_w))
    q, k, v = _split(qkv, (GDN_HEADS * GDN_DK, GDN_HEADS * GDN_DK, GDN_HEADS * GDN_DV))
    q = l2_norm(q.reshape(bsz, t, GDN_HEADS, GDN_DK)) * (GDN_DK ** -0.5)
    k = l2_norm(k.reshape(bsz, t, GDN_HEADS, GDN_DK))
    v = v.reshape(bsz, t, GDN_HEADS, GDN_DV)
    beta = jax.nn.sigmoid(b)
    g = -jnp.exp(a_log.astype(jnp.float32)) * jax.nn.softplus((a + dt_bias).astype(jnp.float32))
    o = gated_delta_rule_chunked(q, k, v, g, beta)
    o = rms_norm(o, norm_g) * jax.nn.silu(z.reshape(bsz, t, GDN_HEADS, GDN_DV))
    return o.reshape(bsz, t, GDN_HEADS * GDN_DV)


def moba_attention(q, k, v, rel_bias):
    bsz, nh, t, hd = q.shape
    nb = -(-t // MOBA_BLOCK)
    tp = nb * MOBA_BLOCK
    pad = ((0, 0), (0, 0), (0, tp - t), (0, 0))
    q, k, v = (jnp.pad(arr, pad) for arr in (q, k, v))
    kb = k.reshape(bsz, nh, nb, MOBA_BLOCK, hd)
    vb = v.reshape(bsz, nh, nb, MOBA_BLOCK, hd)
    kmean = jnp.mean(kb.astype(jnp.float32), axis=3)
    top = min(MOBA_TOPK, nb)
    nq = tp // MOBA_Q_BLOCK
    qcs = jnp.moveaxis(q.reshape(bsz, nh, nq, MOBA_Q_BLOCK, hd), 2, 0)
    bias_hk = rel_bias.T
    bi = jnp.arange(bsz)[:, None, None, None]
    hi = jnp.arange(nh)[None, :, None, None]
    blk_ar = jnp.arange(nb)
    kar = jnp.arange(MOBA_BLOCK)
    scale = hd ** -0.5

    def one_query_block(args):
        qc, ci = args
        q_pos = ci * MOBA_Q_BLOCK + jnp.arange(MOBA_Q_BLOCK)
        own = (ci * MOBA_Q_BLOCK) // MOBA_BLOCK
        gate = jnp.einsum('bhqd,bhnd->bhqn', qc.astype(jnp.float32), kmean)
        gate = jnp.where(blk_ar < own, gate, NEG_INF)
        _, sel = lax.top_k(gate, top)
        valid = sel < own
        kg = kb[bi, hi, sel]
        vg = vb[bi, hi, sel]
        dist_g = q_pos[None, None, :, None, None] - (sel[..., None] * MOBA_BLOCK + kar)
        lg = jnp.einsum('bhqd,bhqnkd->bhqnk', qc, kg).astype(jnp.float32) * scale + bias_hk[hi[..., None], t5_bucket(dist_g)]
        lg = jnp.where(valid[..., None], lg, NEG_INF).reshape(bsz, nh, MOBA_Q_BLOCK, top * MOBA_BLOCK)
        ko = lax.dynamic_index_in_dim(kb, own, axis=2, keepdims=False)
        vo = lax.dynamic_index_in_dim(vb, own, axis=2, keepdims=False)
        dist_o = q_pos[:, None] - (own * MOBA_BLOCK + kar)[None, :]
        lo = jnp.einsum('bhqd,bhkd->bhqk', qc, ko).astype(jnp.float32) * scale + bias_hk[:, t5_bucket(dist_o)]
        lo = jnp.where(dist_o >= 0, lo, NEG_INF)
        p = jax.nn.softmax(jnp.concatenate([lg, lo], axis=-1), axis=-1).astype(v.dtype)
        out = jnp.einsum('bhqk,bhqkd->bhqd', p[..., :top * MOBA_BLOCK], vg.reshape(bsz, nh, MOBA_Q_BLOCK, top * MOBA_BLOCK, hd))
        return out + jnp.einsum('bhqk,bhkd->bhqd', p[..., top * MOBA_BLOCK:], vo)

    out = lax.map(one_query_block, (qcs, jnp.arange(nq)))
    return jnp.moveaxis(out, 0, 2).reshape(bsz, nh, tp, hd)[:, :, :t]


def moba_mixer(qkv, q_norm, k_norm, rel_bias):
    bsz, t, _ = qkv.shape
    qkv = qkv.reshape(bsz, t, 3, ATTN_HEADS, ATTN_HEAD_DIM)
    q = rms_norm(qkv[:, :, 0], q_norm)
    k = rms_norm(qkv[:, :, 1], k_norm)
    v = qkv[:, :, 2]
    q, k, v = (jnp.moveaxis(arr, 2, 1) for arr in (q, k, v))
    o = moba_attention(q, k, v, rel_bias)
    return jnp.moveaxis(o, 1, 2).reshape(bsz, t, ATTN_HEADS * ATTN_HEAD_DIM)


def hybrid_mixer(u, w_in, ssd_conv_w, ssd_conv_b, ssd_dt_bias, ssd_a_log, ssd_d, ssd_norm, w_o_ssd,
                 gdn_conv_w, gdn_dt_bias, gdn_a_log, gdn_norm, w_o_gdn,
                 attn_q_norm, attn_k_norm, rel_bias, w_o_attn, w_out):
    proj = u @ w_in
    ssd_z, ssd_xbc, ssd_dt, gdn_qkv, gdn_z, gdn_a, gdn_b, attn_qkv, gate_logits = _split(proj, IN_SPLITS)
    y_ssd = ssd_mixer(ssd_z, ssd_xbc, ssd_dt, ssd_conv_w, ssd_conv_b, ssd_dt_bias, ssd_a_log, ssd_d, ssd_norm) @ w_o_ssd
    y_gdn = gdn_mixer(gdn_qkv, gdn_z, gdn_a, gdn_b, gdn_conv_w, gdn_dt_bias, gdn_a_log, gdn_norm) @ w_o_gdn
    y_attn = moba_mixer(attn_qkv, attn_q_norm, attn_k_norm, rel_bias) @ w_o_attn
    g_ssd, g_gdn, g_attn = jnp.split(jax.nn.sigmoid(gate_logits), N_BRANCHES, axis=-1)
    return (g_ssd * y_ssd + g_gdn * y_gdn + g_attn * y_attn) @ w_out


def setup_inputs(seed: int = 0) -> dict:
    key = jax.random.key(seed)
    ks = iter(jax.random.split(key, 48))
    L, D = DEPTH, D_MODEL

    def nrm(shape, s):
        return jax.random.normal(next(ks), shape, jnp.float32) * s

    def gain(shape):
        return 1.0 + nrm(shape, 0.02)

    def dt_bias(n):
        dt = jnp.exp(jax.random.uniform(next(ks), (L, n), jnp.float32, minval=math.log(1e-3), maxval=math.log(1e-1)))
        return dt + jnp.log(-jnp.expm1(-dt))

    def a_log(n):
        return jnp.log(jax.random.uniform(next(ks), (L, n), jnp.float32, minval=1.0, maxval=16.0))

    return {
        'x': nrm((BATCH, SEQ, D), 1.0),
        'c': nrm((BATCH, D), 1.0),
        'w_mod': nrm((L, D, N_MOD * D), 0.5 * D ** -0.5),
        'b_mod': nrm((L, N_MOD * D), 0.02),
        'norm_ffn1': gain((L, D)),
        'ffn1_w_gate': nrm((L, D, D_FF), D ** -0.5),
        'ffn1_w_up': nrm((L, D, D_FF), D ** -0.5),
        'ffn1_w_down': nrm((L, D_FF, D), D_FF ** -0.5),
        'norm_mix': gain((L, D)),
        'w_in': nrm((L, D, IN_TOTAL), D ** -0.5),
        'ssd_conv_w': nrm((L, CONV_K, SSD_XBC), CONV_K ** -0.5),
        'ssd_conv_b': nrm((L, SSD_XBC), 0.02),
        'ssd_dt_bias': dt_bias(SSD_HEADS),
        'ssd_a_log': a_log(SSD_HEADS),
        'ssd_d': gain((L, SSD_HEADS)),
        'ssd_norm': gain((L, SSD_D_INNER)),
        'w_o_ssd': nrm((L, SSD_D_INNER, D), SSD_D_INNER ** -0.5),
        'gdn_conv_w': nrm((L, CONV_K, GDN_QKV), CONV_K ** -0.5),
        'gdn_dt_bias': dt_bias(GDN_HEADS),
        'gdn_a_log': a_log(GDN_HEADS),
        'gdn_norm': gain((L, GDN_DV)),
        'w_o_gdn': nrm((L, GDN_HEADS * GDN_DV, D), (GDN_HEADS * GDN_DV) ** -0.5),
        'attn_q_norm': gain((L, ATTN_HEAD_DIM)),
        'attn_k_norm': gain((L, ATTN_HEAD_DIM)),
        'rel_bias': nrm((REL_BUCKETS, ATTN_HEADS), 0.2),
        'w_o_attn': nrm((L, ATTN_HEADS * ATTN_HEAD_DIM, D), (ATTN_HEADS * ATTN_HEAD_DIM) ** -0.5),
        'w_out': nrm((L, D, D), D ** -0.5),
        'norm_ffn2': gain((L, D)),
        'ffn2_w_gate': nrm((L, D, D_FF), D ** -0.5),
        'ffn2_w_up': nrm((L, D, D_FF), D ** -0.5),
        'ffn2_w_down': nrm((L, D_FF, D), D_FF ** -0.5),
    }


def reference(x, c, w_mod, b_mod, norm_ffn1, ffn1_w_gate, ffn1_w_up, ffn1_w_down, norm_mix, w_in,
              ssd_conv_w, ssd_conv_b, ssd_dt_bias, ssd_a_log, ssd_d, ssd_norm, w_o_ssd,
              gdn_conv_w, gdn_dt_bias, gdn_a_log, gdn_norm, w_o_gdn,
              attn_q_norm, attn_k_norm, rel_bias, w_o_attn, w_out,
              norm_ffn2, ffn2_w_gate, ffn2_w_up, ffn2_w_down):
    bsz = x.shape[0]
    h = x
    for l in range(DEPTH):
        mod = (c @ w_mod[l] + b_mod[l]).reshape(bsz, N_MOD, D_MODEL)
        sh1, sc1, g1, sh2, sc2, g2, sh3, sc3, g3 = [mod[:, i, None, :] for i in range(N_MOD)]
        h = h + 0.5 * g1 * swiglu(modulate(h, norm_ffn1[l], sh1, sc1), ffn1_w_gate[l], ffn1_w_up[l], ffn1_w_down[l])
        h = h + g2 * hybrid_mixer(modulate(h, norm_mix[l], sh2, sc2), w_in[l],
                                  ssd_conv_w[l], ssd_conv_b[l], ssd_dt_bias[l], ssd_a_log[l], ssd_d[l], ssd_norm[l], w_o_ssd[l],
                                  gdn_conv_w[l], gdn_dt_bias[l], gdn_a_log[l], gdn_norm[l], w_o_gdn[l],
                                  attn_q_norm[l], attn_k_norm[l], rel_bias, w_o_attn[l], w_out[l])
        h = h + 0.5 * g3 * swiglu(modulate(h, norm_ffn2[l], sh3, sc3), ffn2_w_gate[l], ffn2_w_up[l], ffn2_w_down[l])
    return h
```

```python
import functools
import math

import jax
import jax.numpy as jnp
from jax import lax
from jax.experimental import pallas as pl
from jax.experimental.pallas import tpu as pltpu

F32 = jnp.float32
BF16 = jnp.bfloat16

D_MODEL = 2048
DEPTH = 4
D_FF = 5632
CONV_K = 4
SSD_HEADS = 32
SSD_HEAD_DIM = 64
SSD_D_INNER = SSD_HEADS * SSD_HEAD_DIM
SSD_GROUPS = 4
SSD_STATE = 128
SSD_CHUNK = 128
GDN_HEADS = 16
GDN_DK = 128
GDN_DV = 128
GDN_CHUNK = 64
ATTN_HEADS = 16
ATTN_HEAD_DIM = 128
MOBA_BLOCK = 256
MOBA_TOPK = 3
REL_BUCKETS = 32
REL_MAX_DIST = 4096
N_BRANCHES = 3
N_MOD = 9
EPS = 1e-6
NEG_INF = -1e30

SSD_XBC = SSD_D_INNER + 2 * SSD_GROUPS * SSD_STATE
GDN_QKV = GDN_HEADS * (2 * GDN_DK + GDN_DV)
ATTN_QKV = 3 * ATTN_HEADS * ATTN_HEAD_DIM
IN_SPLITS = (SSD_D_INNER, SSD_XBC, SSD_HEADS, GDN_QKV, GDN_HEADS * GDN_DV, GDN_HEADS, GDN_HEADS, ATTN_QKV,
             N_BRANCHES * D_MODEL)

LANE = 128
V7X_VMEM_BYTES = 64 * 1024 * 1024
N_BIAS_TILES = 14

CONV_N = GDN_QKV + SSD_XBC
PLAIN_N = SSD_D_INNER + GDN_HEADS * GDN_DV + ATTN_QKV + N_BRANCHES * D_MODEL
SMALL_N = 5 * LANE


def _params(semantics, vmem_mb):
    return pltpu.CompilerParams(dimension_semantics=semantics, vmem_limit_bytes=vmem_mb * 1024 * 1024)


def _sigmoid(x):
    return 1.0 / (1.0 + jnp.exp(-x))


def _softplus(x):
    return jnp.maximum(x, 0.0) + jnp.log1p(jnp.exp(-jnp.abs(x)))


def _dot(a, b):
    return jnp.dot(a, b, preferred_element_type=F32)


def _dot_nt(a, b):
    return lax.dot_general(a, b, (((1,), (1,)), ((), ())), preferred_element_type=F32)


def _dot_tn(a, b):
    return lax.dot_general(a, b, (((0,), (0,)), ((), ())), preferred_element_type=F32)


def _dot_hi(a, b):
    return jnp.dot(a, b, preferred_element_type=F32, precision=lax.Precision.HIGHEST)


def _mod_kernel(cb_ref, w_ref, b_ref, o_ref):
    cb = cb_ref[...]
    tn = w_ref.shape[-1]
    for j in range(tn // LANE):
        sl = slice(j * LANE, (j + 1) * LANE)
        o_ref[:, sl] = jnp.sum(w_ref[:, sl] * cb, axis=0, keepdims=True) + b_ref[:, sl]


def _mod_all(c, w_mod, b_mod):
    depth, d, n = w_mod.shape
    tn = 1024
    cb = jnp.broadcast_to(c.reshape(d, 1), (d, LANE))
    return pl.pallas_call(
        _mod_kernel,
        grid=(depth, n // tn),
        in_specs=[pl.BlockSpec((d, LANE), lambda l, j: (0, 0)),
                  pl.BlockSpec((None, d, tn), lambda l, j: (l, 0, j)),
                  pl.BlockSpec((None, 1, tn), lambda l, j: (l, 0, j))],
        out_specs=pl.BlockSpec((None, 1, tn), lambda l, j: (l, 0, j)),
        out_shape=jax.ShapeDtypeStruct((depth, 1, n), F32),
        compiler_params=_params(("parallel", "parallel"), 40),
        name="adaln_mod",
    )(cb, w_mod, b_mod.reshape(depth, 1, n))


def _modulated(x, gn, sh, sc):
    xn = x * lax.rsqrt(jnp.mean(x * x, axis=-1, keepdims=True) + EPS)
    return (xn * gn) * (1.0 + sc) + sh


def _modulate_kernel(h_ref, gn_ref, sh_ref, sc_ref, o_ref):
    o_ref[...] = _modulated(h_ref[...], gn_ref[...], sh_ref[...], sc_ref[...]).astype(o_ref.dtype)


def _modulate(h, gn, sh, sc):
    t, d = h.shape
    tm = 512
    vec = pl.BlockSpec((1, d), lambda i: (0, 0))
    return pl.pallas_call(
        _modulate_kernel,
        grid=(t // tm,),
        in_specs=[pl.BlockSpec((tm, d), lambda i: (i, 0)), vec, vec, vec],
        out_specs=pl.BlockSpec((tm, d), lambda i: (i, 0)),
        out_shape=jax.ShapeDtypeStruct((t, d), BF16),
        compiler_params=_params(("parallel",), 32),
        name="modulate",
    )(h, gn, sh, sc)


def _ffn_kernel(h_ref, gn_ref, sh_ref, sc_ref, g_ref, wg_ref, wu_ref, wd_ref, o_ref, xm_ref):
    f = pl.program_id(1)

    @pl.when(f == 0)
    def _():
        xm_ref[...] = _modulated(h_ref[...], gn_ref[...], sh_ref[...], sc_ref[...]).astype(BF16)
        o_ref[...] = jnp.zeros_like(o_ref)

    xm = xm_ref[...]
    a = _dot(xm, wg_ref[...])
    b = _dot(xm, wu_ref[...])
    hm = ((a * _sigmoid(a)) * b).astype(BF16)
    o_ref[...] += _dot(hm, wd_ref[...])

    @pl.when(f == pl.num_programs(1) - 1)
    def _():
        o_ref[...] = h_ref[...] + (0.5 * g_ref[...]) * o_ref[...]


def _ffn(h, gn, sh, sc, g, wg, wu, wd, l):
    t, d = h.shape
    ff = wg.shape[-1]
    tm, tf = 512, 512
    vec = pl.BlockSpec((1, d), lambda i, f: (0, 0))
    return pl.pallas_call(
        _ffn_kernel,
        grid=(t // tm, ff // tf),
        in_specs=[pl.BlockSpec((tm, d), lambda i, f: (i, 0)), vec, vec, vec, vec,
                  pl.BlockSpec((None, d, tf), lambda i, f: (l, 0, f)),
                  pl.BlockSpec((None, d, tf), lambda i, f: (l, 0, f)),
                  pl.BlockSpec((None, tf, d), lambda i, f: (l, f, 0))],
        out_specs=pl.BlockSpec((tm, d), lambda i, f: (i, 0)),
        out_shape=jax.ShapeDtypeStruct((t, d), F32),
        scratch_shapes=[pltpu.VMEM((tm, d), BF16)],
        compiler_params=_params(("parallel", "arbitrary"), 48),
        name="ffn",
    )(h, gn, sh, sc, g, wg, wu, wd)


def _mm_kernel(x_ref, w_ref, o_ref):
    o_ref[...] = _dot(x_ref[...], w_ref[...]).astype(o_ref.dtype)


def _matmul(x, w, l, tn, tm=1024, out_dtype=F32):
    t, k = x.shape
    n = w.shape[-1]
    return pl.pallas_call(
        _mm_kernel,
        grid=(n // tn, t // tm),
        in_specs=[pl.BlockSpec((tm, k), lambda j, i: (i, 0)),
                  pl.BlockSpec((None, k, tn), lambda j, i: (l, 0, j))],
        out_specs=pl.BlockSpec((tm, tn), lambda j, i: (i, j)),
        out_shape=jax.ShapeDtypeStruct((t, n), out_dtype),
        compiler_params=_params(("parallel", "parallel"), 48),
        name="proj",
    )(x, w)


def _mm_conv_kernel(x_ref, w_ref, cw_ref, cb_ref, o_ref, carry_ref):
    i = pl.program_id(1)

    @pl.when(i == 0)
    def _():
        carry_ref[...] = jnp.zeros_like(carry_ref)

    raw = _dot(x_ref[...], w_ref[...])
    tm = raw.shape[0]
    prev = carry_ref[...]
    cw = cw_ref[...]
    rows8 = lax.broadcasted_iota(jnp.int32, (8, raw.shape[1]), 0)
    y = cb_ref[...] + cw[CONV_K - 1:CONV_K, :] * raw
    y_head = cb_ref[...] + cw[CONV_K - 1:CONV_K, :] * raw[0:8]
    for s in range(1, CONV_K):
        wk = cw[CONV_K - 1 - s:CONV_K - s, :]
        rolled = pltpu.roll(raw, s, 0)
        y = y + wk * rolled
        head = jnp.where(rows8 < s, pltpu.roll(prev, s, 0), rolled[0:8])
        y_head = y_head + wk * head
    y = jnp.concatenate([y_head, y[8:]], axis=0)
    o_ref[...] = y * _sigmoid(y)
    carry_ref[...] = raw[tm - 8:tm]


def _matmul_conv(x, w, cw, cb, l, tn=1024, tm=1024):
    t, k = x.shape
    n = w.shape[-1]
    return pl.pallas_call(
        _mm_conv_kernel,
        grid=(n // tn, t // tm),
        in_specs=[pl.BlockSpec((tm, k), lambda j, i: (i, 0)),
                  pl.BlockSpec((None, k, tn), lambda j, i: (l, 0, j)),
                  pl.BlockSpec((None, CONV_K, tn), lambda j, i: (l, 0, j)),
                  pl.BlockSpec((None, 1, tn), lambda j, i: (l, 0, j))],
        out_specs=pl.BlockSpec((tm, tn), lambda j, i: (i, j)),
        out_shape=jax.ShapeDtypeStruct((t, n), F32),
        scratch_shapes=[pltpu.VMEM((8, tn), F32)],
        compiler_params=_params(("parallel", "arbitrary"), 48),
        name="proj_conv",
    )(x, w, cw, cb)


def _ssd_kernel(xbc_ref, z_ref, dt_ref, dtb_ref, alog_ref, dskip_ref, ng_ref, o_ref, state_ref):
    c = pl.program_id(0)

    @pl.when(c == 0)
    def _():
        state_ref[...] = jnp.zeros_like(state_ref)

    L = SSD_CHUNK
    P = SSD_HEAD_DIM
    gw = SSD_D_INNER // SSD_GROUPS
    hpg = SSD_HEADS // SSD_GROUPS
    dt = _softplus(dt_ref[...] + dtb_ref[...])
    a = -jnp.exp(alog_ref[...])
    da = dt * a
    r_i = lax.broadcasted_iota(jnp.int32, (L, L), 0)
    c_i = lax.broadcasted_iota(jnp.int32, (L, L), 1)
    tri = r_i >= c_i
    acs = _dot_hi(tri.astype(F32), da)
    acs_t = acs.T
    last = acs[L - 1:L, :]
    e_acs = jnp.exp(acs)
    e_dec = jnp.exp(last - acs)
    e_last = jnp.exp(last)
    lane = lax.broadcasted_iota(jnp.int32, (L, LANE), 1)
    lo_half = lane < P

    def pair_bcast(v, h):
        rows = v.shape[0]
        return jnp.where(lo_half[:rows], jnp.broadcast_to(v[:, h:h + 1], (rows, LANE)),
                         jnp.broadcast_to(v[:, h + 1:h + 2], (rows, LANE)))

    for g in range(SSD_GROUPS):
        bg = xbc_ref[:, SSD_D_INNER + g * SSD_STATE:SSD_D_INNER + (g + 1) * SSD_STATE].astype(BF16)
        cg = xbc_ref[:, SSD_D_INNER + SSD_GROUPS * SSD_STATE + g * SSD_STATE:
                     SSD_D_INNER + SSD_GROUPS * SSD_STATE + (g + 1) * SSD_STATE].astype(BF16)
        cb = _dot_nt(cg, bg)
        s_in = state_ref[g]
        y_off = _dot(cg, s_in.astype(BF16))
        xdec_parts = []
        dec_parts = []
        y_parts = []
        for hp in range(hpg // 2):
            h = g * hpg + 2 * hp
            ch = slice(h * P, (h + 2) * P)
            xs = xbc_ref[:, ch]
            xdt = xs * pair_bcast(dt, h)
            gmats = []
            for hh in (h, h + 1):
                diff = acs[:, hh:hh + 1] - acs_t[hh:hh + 1, :]
                lm = jnp.exp(jnp.where(tri, diff, -jnp.inf))
                gmats.append((lm * cb).astype(BF16))
            lhs = jnp.concatenate(gmats, axis=1)
            rhs = jnp.concatenate([jnp.where(lo_half, xdt, 0.0), jnp.where(lo_half, 0.0, xdt)], axis=0).astype(BF16)
            y = _dot(lhs, rhs)
            y = y + y_off[:, 2 * hp * P:(2 * hp + 2) * P] * pair_bcast(e_acs, h)
            y = y + xs * dskip_ref[:, ch]
            zz = z_ref[:, ch]
            y_parts.append(y * (zz * _sigmoid(zz)))
            xdec_parts.append((xdt * pair_bcast(e_dec, h)).astype(BF16))
            dec_parts.append(pair_bcast(e_last, h))
        xdec = jnp.concatenate(xdec_parts, axis=1)
        dec = jnp.concatenate(dec_parts, axis=1)
        state_ref[g] = s_in * dec + _dot_tn(bg, xdec)
        ssq = sum(jnp.sum(y * y, axis=-1, keepdims=True) for y in y_parts)
        inv_rms = lax.rsqrt(ssq * (1.0 / gw) + EPS)
        for i, y in enumerate(y_parts):
            ch = slice(g * gw + i * LANE, g * gw + (i + 1) * LANE)
            o_ref[:, ch] = (y * inv_rms * ng_ref[:, ch]).astype(o_ref.dtype)


def _ssd(conv, plain, small, dtb, alog, dskip, ng):
    t = conv.shape[0]
    L = SSD_CHUNK
    xbc_blk = GDN_QKV // SSD_XBC
    vec = lambda n: pl.BlockSpec((1, n), lambda c: (0, 0))
    return pl.pallas_call(
        _ssd_kernel,
        grid=(t // L,),
        in_specs=[pl.BlockSpec((L, SSD_XBC), lambda c: (c, xbc_blk)),
                  pl.BlockSpec((L, SSD_D_INNER), lambda c: (c, 0)),
                  pl.BlockSpec((L, LANE), lambda c: (c, 0)),
                  vec(LANE), vec(LANE), vec(SSD_D_INNER), vec(SSD_D_INNER)],
        out_specs=pl.BlockSpec((L, SSD_D_INNER), lambda c: (c, 0)),
        out_shape=jax.ShapeDtypeStruct((t, SSD_D_INNER), BF16),
        scratch_shapes=[pltpu.VMEM((SSD_GROUPS, SSD_STATE, SSD_D_INNER // SSD_GROUPS), F32)],
        compiler_params=_params(("arbitrary",), 32),
        name="ssd",
    )(conv, plain, small, dtb, alog, dskip, ng)


def _gdn_kernel(qkv_ref, z_ref, ae_ref, ao_ref, be_ref, bo_ref, dtb_ref, alog_ref, ng_ref, o_ref, state_ref):
    c = pl.program_id(0)

    @pl.when(c == 0)
    def _():
        state_ref[...] = jnp.zeros_like(state_ref)

    L = GDN_CHUNK
    R = 2 * L
    n_pairs = GDN_HEADS // 2
    row1 = lax.broadcasted_iota(jnp.int32, (R, LANE), 0)
    top = row1 < L
    a2 = jnp.concatenate([ae_ref[...], ao_ref[...]], axis=0)
    b2 = jnp.concatenate([be_ref[...], bo_ref[...]], axis=0)
    dtb2 = jnp.where(top, dtb_ref[0:1, :], dtb_ref[1:2, :])
    alog2 = jnp.where(top, alog_ref[0:1, :], alog_ref[1:2, :])
    beta2 = _sigmoid(b2)
    g2 = -jnp.exp(alog2) * _softplus(a2 + dtb2)
    r_i = lax.broadcasted_iota(jnp.int32, (R, R), 0)
    c_i = lax.broadcasted_iota(jnp.int32, (R, R), 1)
    same = (r_i < L) == (c_i < L)
    incl = same & (r_i >= c_i)
    strict = same & (r_i > c_i)
    gc2 = _dot_hi(incl.astype(F32), g2)
    gc2_t = gc2.T
    gl2 = jnp.where(top, gc2[L - 1:L, :], gc2[R - 1:R, :])
    e_gc = jnp.exp(gc2)
    e_rem = jnp.exp(gl2 - gc2)
    e_gl = jnp.exp(gl2)
    eye = (r_i == c_i).astype(F32)
    merge_masks = []
    for lb in range(int(math.log2(L))):
        r_blk = jnp.right_shift(r_i, lb)
        c_blk = jnp.right_shift(c_i, lb)
        merge_masks.append((r_blk == c_blk + 1) & (jnp.bitwise_and(r_blk, 1) == 1))
    row2 =lax.broadcasted_iota(jnp.int32, (R, 2 * LANE), 0)
    col2 = lax.broadcasted_iota(jnp.int32, (R, 2 * LANE), 1)
    diag2 = (row2 < L) == (col2 < LANE)
    srow = lax.broadcasted_iota(jnp.int32, (2 * GDN_DK, LANE), 0)

    def stack(ref, base, p):
        lo = base + 2 * p * LANE
        return jnp.concatenate([ref[:, lo:lo + LANE], ref[:, lo + LANE:lo + 2 * LANE]], axis=0)

    def blockdiag(x):
        return jnp.where(diag2, jnp.concatenate([x, x], axis=1), 0.0).astype(BF16)

    def col(v, p):
        return jnp.broadcast_to(v[:, p:p + 1], (R, LANE))

    for p in range(n_pairs):
        q = stack(qkv_ref, 0, p)
        k = stack(qkv_ref, GDN_HEADS * GDN_DK, p)
        v = stack(qkv_ref, 2 * GDN_HEADS * GDN_DK, p)
        qn = q * lax.rsqrt(jnp.sum(q * q, axis=-1, keepdims=True) + EPS) * (GDN_DK ** -0.5)
        kn = k * lax.rsqrt(jnp.sum(k * k, axis=-1, keepdims=True) + EPS)
        beta = col(beta2, p)
        diff = gc2[:, p:p + 1] - gc2_t[p:p + 1, :]
        decay = jnp.exp(jnp.where(incl, diff, -jnp.inf))
        kb = kn * beta
        kn_b = kn.astype(BF16)
        a_mat = jnp.where(strict, _dot_nt(kb.astype(BF16), kn_b) * decay, 0.0)
        inv = eye - jnp.where(merge_masks[0], a_mat, 0.0)
        for mk in merge_masks[1:]:
            inv_b = inv.astype(BF16)
            e = jnp.where(mk, a_mat, 0.0).astype(BF16)
            inv = inv - _dot(_dot(inv_b, e).astype(BF16), inv_b)
        rhs = jnp.concatenate([v * beta, kb * col(e_gc, p)], axis=1).astype(BF16)
        sol = _dot(inv.astype(BF16), rhs)
        u = sol[:, :GDN_DV]
        w = sol[:, GDN_DV:]
        qk = jnp.where(incl, _dot_nt(qn.astype(BF16), kn_b) * decay, 0.0)
        q_dec = qn * col(e_gc, p)
        k_dec = kn * col(e_rem, p)
        s = state_ref[p]
        s_b = s.astype(BF16)
        v_c = u - _dot(blockdiag(w), s_b)
        v_cb = v_c.astype(BF16)
        o = _dot(blockdiag(q_dec), s_b) + _dot(qk.astype(BF16), v_cb)
        sdec = jnp.where(srow < GDN_DK, jnp.broadcast_to(e_gl[0:1, p:p + 1], (2 * GDN_DK, LANE)),
                         jnp.broadcast_to(e_gl[R - 1:R, p:p + 1], (2 * GDN_DK, LANE)))
        state_ref[p] = s * sdec + _dot_tn(blockdiag(k_dec), v_cb)
        z = stack(z_ref, 0, p)
        on = o * lax.rsqrt(jnp.mean(o * o, axis=-1, keepdims=True) + EPS) * ng_ref[...]
        res = (on * (z * _sigmoid(z))).astype(o_ref.dtype)
        o_ref[:, 2 * p * LANE:(2 * p + 1) * LANE] = res[:L]
        o_ref[:, (2 * p + 1) * LANE:(2 * p + 2) * LANE] = res[L:]


def _gdn(conv, plain, small, dtb, alog, ng):
    t = conv.shape[0]
    L = GDN_CHUNK
    dz = GDN_HEADS * GDN_DV
    sm = lambda j: pl.BlockSpec((L, LANE), lambda c: (c, j))
    return pl.pallas_call(
        _gdn_kernel,
        grid=(t // L,),
        in_specs=[pl.BlockSpec((L, GDN_QKV), lambda c: (c, 0)),
                  pl.BlockSpec((L, dz), lambda c: (c, SSD_D_INNER // dz)),
                  sm(1), sm(2), sm(3), sm(4),
                  pl.BlockSpec((2, LANE), lambda c: (0, 0)),
                  pl.BlockSpec((2, LANE), lambda c: (0, 0)),
                  pl.BlockSpec((1, GDN_DV), lambda c: (0, 0))],
        out_specs=pl.BlockSpec((L, dz), lambda c: (c, 0)),
        out_shape=jax.ShapeDtypeStruct((t, dz), BF16),
        scratch_shapes=[pltpu.VMEM((GDN_HEADS // 2, 2 * GDN_DK, GDN_DV), F32)],
        compiler_params=_params(("arbitrary",), 32),
        name="gdn",
    )(conv, plain, small, small, small, small, dtb, alog, ng)


def _attn_kernel(q_ref, k_ref, v_ref, qg_ref, kg_ref, bias_ref, o_ref, kaug_ref, vb_ref, kmean_ref):
    qt = pl.program_id(1)
    B = MOBA_BLOCK
    t = k_ref.shape[0]
    nb = t // B

    @pl.when(qt == 0)
    def _():
        lane_b = lax.broadcasted_iota(jnp.int32, (B, LANE), 1)

        for b in range(nb):
            rows = slice(b * B, (b + 1) * B)
            k = k_ref[rows, :]
            kn = k * lax.rsqrt(jnp.mean(k * k, axis=-1, keepdims=True) + EPS) * kg_ref[...]
            kaug_ref[rows, 0:LANE] = kn.astype(BF16)
            kaug_ref[rows, LANE:2 * LANE] = jnp.where(lane_b == b, 1.0, 0.0).astype(BF16)
            vb_ref[rows, :] = v_ref[rows, :].astype(BF16)
            kmean_ref[b:b + 1, :] = jnp.mean(kn, axis=0, keepdims=True)

    q = q_ref[...]
    qn = q * lax.rsqrt(jnp.mean(q * q, axis=-1, keepdims=True) + EPS) * qg_ref[...]
    own = qt
    gate = lax.dot_general(kmean_ref[...], qn, (((1,), (1,)), ((), ())), preferred_element_type=F32,
                           precision=lax.Precision.HIGHEST)
    blk = lax.broadcasted_iota(jnp.int32, gate.shape, 0)
    past = blk < own
    gate = jnp.where(past, gate, NEG_INF)
    chosen = blk == own
    blk_f = blk.astype(F32)
    for _ in range(MOBA_TOPK):
        best = jnp.max(gate, axis=0, keepdims=True)
        first = jnp.min(jnp.where(gate == best, blk_f, float(nb)), axis=0, keepdims=True)
        hit = blk_f == first
        chosen = chosen | (hit & past)
        gate = jnp.where(hit, -jnp.inf, gate)
    selb_t = jnp.where(chosen, 0.0, NEG_INF)
    selb_t = jnp.concatenate([selb_t, jnp.zeros((LANE - nb, B), F32)], axis=0)
    selb = selb_t.T
    q_aug = jnp.concatenate([(qn * (ATTN_HEAD_DIM ** -0.5)).astype(BF16), selb.astype(BF16)], axis=1)

    def pair(j, carry):
        m, l, acc = carry
        rows = pl.ds(pl.multiple_of(j * B, B), B)
        s = _dot_nt(q_aug, kaug_ref[rows, :])
        s = s + bias_ref[jnp.minimum(own - j, N_BIAS_TILES - 1)]
        m_new = jnp.maximum(m, jnp.max(s, axis=-1, keepdims=True))
        p = jnp.exp(s - m_new)
        alpha = jnp.exp(m - m_new)
        l = alpha * l + jnp.sum(p, axis=-1, keepdims=True)
        acc = alpha * acc + _dot(p.astype(BF16), vb_ref[rows, :])
        return m_new, l, acc

    init = (jnp.full((B, 1), NEG_INF, F32), jnp.zeros((B, 1), F32), jnp.zeros((B, ATTN_HEAD_DIM), F32))
    carry = pair(own, init)
    m, l, acc = lax.fori_loop(0, own, pair, carry)
    o_ref[...] = (acc / l).astype(o_ref.dtype)


def _attn(plain, qg, kg, bias_tiles):
    t = plain.shape[0]
    B = MOBA_BLOCK
    hd = ATTN_HEAD_DIM
    base = (SSD_D_INNER + GDN_HEADS * GDN_DV) // hd
    return pl.pallas_call(
        _attn_kernel,
        grid=(ATTN_HEADS, t // B),
        in_specs=[pl.BlockSpec((B, hd), lambda h, i: (i, base + h)),
                  pl.BlockSpec((t, hd), lambda h, i: (0, base + ATTN_HEADS + h)),
                  pl.BlockSpec((t, hd), lambda h, i: (0, base + 2 * ATTN_HEADS + h)),
                  pl.BlockSpec((1, hd), lambda h, i: (0, 0)),
                  pl.BlockSpec((1, hd), lambda h, i: (0, 0)),
                  pl.BlockSpec((None, N_BIAS_TILES, B, B), lambda h, i: (h, 0, 0, 0))],
        out_specs=pl.BlockSpec((B, hd), lambda h, i: (i, h)),
        out_shape=jax.ShapeDtypeStruct((t, ATTN_HEADS * hd), BF16),
        scratch_shapes=[pltpu.VMEM((t, 2 * LANE), BF16), pltpu.VMEM((t, hd), BF16),
                        pltpu.VMEM((t // B, hd), F32)],
        compiler_params=_params(("arbitrary", "arbitrary"), 48),
        name="moba",
    )(plain, plain, plain, qg, kg, bias_tiles)


def _bias_tiles(rel_bias, t):
    B = MOBA_BLOCK
    dist = jnp.arange(t, dtype=jnp.int32)
    exact = REL_BUCKETS // 2
    nf = jnp.maximum(dist, 1).astype(F32)
    large = exact + (jnp.log(nf / exact) / math.log(REL_MAX_DIST / exact) * (REL_BUCKETS - exact)).astype(jnp.int32)
    bucket = jnp.where(dist < exact, dist, jnp.minimum(large, REL_BUCKETS - 1))
    by_dist = rel_bias.T[:, bucket]
    i = jnp.arange(B)[:, None]
    j = jnp.arange(B)[None, :]
    tiles = []
    for d in range(N_BIAS_TILES - 1):
        idx = jnp.clip(d * B + i - j, 0, t - 1)
        tile = by_dist[:, idx]
        if d == 0:
            tile = jnp.where(i >= j, tile, NEG_INF)
        tiles.append(tile)
    far = jnp.broadcast_to(rel_bias.T[:, REL_BUCKETS - 1][:, None, None], tiles[0].shape)
    tiles.append(far)
    return jnp.stack(tiles, axis=1)


def _merge_kernel(y1_ref, y2_ref, y3_ref, w1_ref, w2_ref, w3_ref, g1_ref, g2_ref, g3_ref, o_ref):
    acc = _sigmoid(g1_ref[...]) * _dot(y1_ref[...], w1_ref[...])
    acc = acc + _sigmoid(g2_ref[...]) * _dot(y2_ref[...], w2_ref[...])
    acc = acc + _sigmoid(g3_ref[...]) * _dot(y3_ref[...], w3_ref[...])
    o_ref[...] = acc.astype(o_ref.dtype)


def _merge(y_ssd, y_gdn, y_attn, w1, w2, w3, plain, l):
    t, d = y_ssd.shape
    tm, tn = 512, 512
    gbase = (SSD_D_INNER + GDN_HEADS * GDN_DV + ATTN_QKV) // tn
    yspec = pl.BlockSpec((tm, d), lambda i, j: (i, 0))
    wspec = pl.BlockSpec((None, d, tn), lambda i, j: (l, 0, j))
    gspec = lambda b: pl.BlockSpec((tm, tn), lambda i, j: (i, gbase + b * (D_MODEL // tn) + j))
    return pl.pallas_call(
        _merge_kernel,
        grid=(t // tm, D_MODEL // tn),
        in_specs=[yspec, yspec, yspec, wspec, wspec, wspec, gspec(0), gspec(1), gspec(2)],
        out_specs=pl.BlockSpec((tm, tn), lambda i, j: (i, j)),
        out_shape=jax.ShapeDtypeStruct((t, D_MODEL), BF16),
        compiler_params=_params(("parallel", "parallel"), 48),
        name="merge",
    )(y_ssd, y_gdn, y_attn, w1, w2, w3, plain, plain, plain)


def _out_kernel(x_ref, w_ref, h_ref, g_ref, o_ref):
    o_ref[...] = h_ref[...] + g_ref[...] * _dot(x_ref[...], w_ref[...])


def _out_proj(x, w, h, g, l):
    t, k = x.shape
    n = w.shape[-1]
    tm, tn = 1024, 1024
    return pl.pallas_call(
        _out_kernel,
        grid=(n // tn, t // tm),
        in_specs=[pl.BlockSpec((tm, k), lambda j, i: (i, 0)),
                  pl.BlockSpec((None, k, tn), lambda j, i: (l, 0, j)),
                  pl.BlockSpec((tm, tn), lambda j, i: (i, j)),
                  pl.BlockSpec((1, tn), lambda j, i: (0, j))],
        out_specs=pl.BlockSpec((tm, tn), lambda j, i: (i, j)),
        out_shape=jax.ShapeDtypeStruct((t, n), F32),
        compiler_params=_params(("parallel", "parallel"), 48),
        name="out_proj",
    )(x, w, h, g)


def _pad_lanes(w, n=LANE):
    return jnp.pad(w, [(0, 0)] * (w.ndim - 1) + [(0, n - w.shape[-1])])


def _split_in(w_in):
    cuts = []
    lo = 0
    for s in IN_SPLITS:
        cuts.append(w_in[..., lo:lo + s])
        lo += s
    return cuts


def kernel(x, c, w_mod, b_mod, norm_ffn1, ffn1_w_gate, ffn1_w_up, ffn1_w_down, norm_mix, w_in, ssd_conv_w, ssd_conv_b, ssd_dt_bias, ssd_a_log, ssd_d, ssd_norm, w_o_ssd, gdn_conv_w, gdn_dt_bias, gdn_a_log, gdn_norm, w_o_gdn, attn_q_norm, attn_k_norm, rel_bias, w_o_attn, w_out, norm_ffn2, ffn2_w_gate, ffn2_w_up, ffn2_w_down):
    bsz, t, d = x.shape
    assert bsz == 1 and d == D_MODEL and t % 1024 == 0 and t // MOBA_BLOCK <= LANE
    depth = w_mod.shape[0]

    ssd_z, ssd_xbc, ssd_dt, gdn_qkv, gdn_z, gdn_a, gdn_b, attn_qkv, gates = _split_in(w_in)
    w_conv = jnp.concatenate([gdn_qkv, ssd_xbc], axis=-1).astype(BF16)
    w_plain = jnp.concatenate([ssd_z, gdn_z, attn_qkv, gates], axis=-1).astype(BF16)
    w_small = jnp.concatenate([_pad_lanes(ssd_dt), _pad_lanes(gdn_a[..., 0::2]), _pad_lanes(gdn_a[..., 1::2]),
                               _pad_lanes(gdn_b[..., 0::2]), _pad_lanes(gdn_b[..., 1::2])], axis=-1).astype(BF16)
    conv_w = jnp.concatenate([gdn_conv_w, ssd_conv_w], axis=-1)
    conv_b = jnp.concatenate([jnp.zeros((depth, GDN_QKV), F32), ssd_conv_b], axis=-1).reshape(depth, 1, CONV_N)
    ssd_dtb = _pad_lanes(ssd_dt_bias)
    ssd_alog = _pad_lanes(ssd_a_log)
    ssd_dskip = jnp.repeat(ssd_d, SSD_HEAD_DIM, axis=-1)
    gdn_dtb = jnp.stack([_pad_lanes(gdn_dt_bias[:, 0::2]), _pad_lanes(gdn_dt_bias[:, 1::2])], axis=1)
    gdn_alog = jnp.stack([_pad_lanes(gdn_a_log[:, 0::2]), _pad_lanes(gdn_a_log[:, 1::2])], axis=1)
    f1g, f1u, f1d = ffn1_w_gate.astype(BF16), ffn1_w_up.astype(BF16), ffn1_w_down.astype(BF16)
    f2g, f2u, f2d = ffn2_w_gate.astype(BF16), ffn2_w_up.astype(BF16), ffn2_w_down.astype(BF16)
    wo_ssd, wo_gdn, wo_attn, wo = (w_o_ssd.astype(BF16), w_o_gdn.astype(BF16), w_o_attn.astype(BF16),
                                   w_out.astype(BF16))
    bias_tiles = _bias_tiles(rel_bias, t)

    mod = _mod_all(c, w_mod, b_mod).reshape(depth, N_MOD, d)
    h = x.reshape(t, d)
    for l in range(depth):
        sh1, sc1, g1, sh2, sc2, g2, sh3, sc3, g3 = [mod[l, i][None, :] for i in range(N_MOD)]
        h = _ffn(h, norm_ffn1[l][None, :], sh1, sc1, g1, f1g, f1u, f1d, l)
        u = _modulate(h, norm_mix[l][None, :], sh2, sc2)
        conv = _matmul_conv(u, w_conv, conv_w, conv_b, l)
        plain = _matmul(u, w_plain, l, tn=1024)
        small = _matmul(u, w_small, l, tn=SMALL_N)
        y_ssd = _ssd(conv, plain, small, ssd_dtb[l][None, :], ssd_alog[l][None, :], ssd_dskip[l][None, :],
                     ssd_norm[l][None, :])
        y_gdn = _gdn(conv, plain, small, gdn_dtb[l], gdn_alog[l], gdn_norm[l][None, :])
        y_attn = _attn(plain, attn_q_norm[l][None, :], attn_k_norm[l][None, :], bias_tiles)
        merged = _merge(y_ssd, y_gdn, y_attn, wo_ssd, wo_gdn, wo_attn, plain, l)
        h = _out_proj(merged, wo, h, g2, l)
        h = _ffn(h, norm_ffn2[l][None, :], sh3, sc3, g3, f2g, f2u, f2d, l)
    return h.reshape(bsz, t, d)
```

```python
import functools
import math

import jax
import jax.numpy as jnp
from jax import lax
from jax.experimental import pallas as pl
from jax.experimental.pallas import tpu as pltpu

F32 = jnp.float32
BF16 = jnp.bfloat16

D_MODEL = 2048
DEPTH = 4
D_FF = 5632
CONV_K = 4
SSD_HEADS = 32
SSD_HEAD_DIM = 64
SSD_D_INNER = SSD_HEADS * SSD_HEAD_DIM
SSD_GROUPS = 4
SSD_STATE = 128
SSD_CHUNK = 128
GDN_HEADS = 16
GDN_DK = 128
GDN_DV = 128
GDN_CHUNK = 64
ATTN_HEADS = 16
ATTN_HEAD_DIM = 128
MOBA_BLOCK = 256
MOBA_TOPK = 3
REL_BUCKETS = 32
REL_MAX_DIST = 4096
N_BRANCHES = 3
N_MOD = 9
EPS = 1e-6
NEG_INF = -1e30

SSD_XBC = SSD_D_INNER + 2 * SSD_GROUPS * SSD_STATE
GDN_QKV = GDN_HEADS * (2 * GDN_DK + GDN_DV)
ATTN_QKV = 3 * ATTN_HEADS * ATTN_HEAD_DIM
IN_SPLITS = (SSD_D_INNER, SSD_XBC, SSD_HEADS, GDN_QKV, GDN_HEADS * GDN_DV, GDN_HEADS, GDN_HEADS, ATTN_QKV,
             N_BRANCHES * D_MODEL)

LANE = 128
V7X_VMEM_BYTES = 64 * 1024 * 1024
N_BIAS_TILES = 15
ATTN_UNROLL = 8
LOG2E = 1.4426950408889634

CONV_N = GDN_QKV + SSD_XBC
PLAIN_N = SSD_D_INNER + GDN_HEADS * GDN_DV + ATTN_QKV + N_BRANCHES * D_MODEL
SMALL_N = 5 * LANE


def _params(semantics, vmem_mb):
    return pltpu.CompilerParams(dimension_semantics=semantics, vmem_limit_bytes=vmem_mb * 1024 * 1024)


def _sigmoid(x):
    return 1.0 / (1.0 + jnp.exp(-x))


def _softplus(x):
    return jnp.maximum(x, 0.0) + jnp.log1p(jnp.exp(-jnp.abs(x)))


def _dot(a, b):
    return jnp.dot(a, b, preferred_element_type=F32)


def _dot_nt(a, b):
    return lax.dot_general(a, b, (((1,), (1,)), ((), ())), preferred_element_type=F32)


def _dot_tn(a, b):
    return lax.dot_general(a, b, (((0,), (0,)), ((), ())), preferred_element_type=F32)


def _dot_hi(a, b):
    return jnp.dot(a, b, preferred_element_type=F32, precision=lax.Precision.HIGHEST)


def _mod_kernel(cb_ref, w_ref, b_ref, o_ref):
    cb = cb_ref[...]
    tn = w_ref.shape[-1]
    for j in range(tn // LANE):
        sl = slice(j * LANE, (j + 1) * LANE)
        o_ref[:, sl] = jnp.sum(w_ref[:, sl] * cb, axis=0, keepdims=True) + b_ref[:, sl]


def _mod_all(c, w_mod, b_mod):
    depth, d, n = w_mod.shape
    tn = 1024
    cb = jnp.broadcast_to(c.reshape(d, 1), (d, LANE))
    return pl.pallas_call(
        _mod_kernel,
        grid=(depth, n // tn),
        in_specs=[pl.BlockSpec((d, LANE), lambda l, j: (0, 0)),
                  pl.BlockSpec((None, d, tn), lambda l, j: (l, 0, j)),
                  pl.BlockSpec((None, 1, tn), lambda l, j: (l, 0, j))],
        out_specs=pl.BlockSpec((None, 1, tn), lambda l, j: (l, 0, j)),
        out_shape=jax.ShapeDtypeStruct((depth, 1, n), F32),
        compiler_params=_params(("parallel", "parallel"), 40),
        name="adaln_mod",
    )(cb, w_mod, b_mod.reshape(depth, 1, n))


def _modulated(x, gn, sh, sc):
    xn = x * lax.rsqrt(jnp.mean(x * x, axis=-1, keepdims=True) + EPS)
    return (xn * gn) * (1.0 + sc) + sh


def _modulate_kernel(h_ref, gn_ref, sh_ref, sc_ref, o_ref):
    o_ref[...] = _modulated(h_ref[...], gn_ref[...], sh_ref[...], sc_ref[...]).astype(o_ref.dtype)


def _modulate(h, gn, sh, sc):
    t, d = h.shape
    tm = 512
    vec = pl.BlockSpec((1, d), lambda i: (0, 0))
    return pl.pallas_call(
        _modulate_kernel,
        grid=(t // tm,),
        in_specs=[pl.BlockSpec((tm, d), lambda i: (i, 0)), vec, vec, vec],
        out_specs=pl.BlockSpec((tm, d), lambda i: (i, 0)),
        out_shape=jax.ShapeDtypeStruct((t, d), BF16),
        compiler_params=_params(("parallel",), 32),
        name="modulate",
    )(h, gn, sh, sc)


def _ffn_kernel(h_ref, gn_ref, sh_ref, sc_ref, g_ref, wg_ref, wu_ref, wd_ref, o_ref, xm_ref):
    f = pl.program_id(1)

    @pl.when(f == 0)
    def _():
        xm_ref[...] = _modulated(h_ref[...], gn_ref[...], sh_ref[...], sc_ref[...]).astype(BF16)
        o_ref[...] = jnp.zeros_like(o_ref)

    xm = xm_ref[...]
    a = _dot(xm, wg_ref[...])
    b = _dot(xm, wu_ref[...])
    hm = ((a * _sigmoid(a)) * b).astype(BF16)
    o_ref[...] += _dot(hm, wd_ref[...])

    @pl.when(f == pl.num_programs(1) - 1)
    def _():
        o_ref[...] = h_ref[...] + (0.5 * g_ref[...]) * o_ref[...]


def _ffn(h, gn, sh, sc, g, wg, wu, wd, l):
    t, d = h.shape
    ff = wg.shape[-1]
    tm, tf = 512, 512
    vec = pl.BlockSpec((1, d), lambda i, f: (0, 0))
    return pl.pallas_call(
        _ffn_kernel,
        grid=(t // tm, ff // tf),
        in_specs=[pl.BlockSpec((tm, d), lambda i, f: (i, 0)), vec, vec, vec, vec,
                  pl.BlockSpec((None, d, tf), lambda i, f: (l, 0, f)),
                  pl.BlockSpec((None, d, tf), lambda i, f: (l, 0, f)),
                  pl.BlockSpec((None, tf, d), lambda i, f: (l, f, 0))],
        out_specs=pl.BlockSpec((tm, d), lambda i, f: (i, 0)),
        out_shape=jax.ShapeDtypeStruct((t, d), F32),
        scratch_shapes=[pltpu.VMEM((tm, d), BF16)],
        compiler_params=_params(("parallel", "arbitrary"), 48),
        name="ffn",
    )(h, gn, sh, sc, g, wg, wu, wd)


def _mm_kernel(x_ref, w_ref, o_ref):
    o_ref[...] = _dot(x_ref[...], w_ref[...]).astype(o_ref.dtype)


def _matmul(x, w, l, tn, tm=1024, out_dtype=F32):
    t, k = x.shape
    n = w.shape[-1]
    return pl.pallas_call(
        _mm_kernel,
        grid=(n // tn, t // tm),
        in_specs=[pl.BlockSpec((tm, k), lambda j, i: (i, 0)),
                  pl.BlockSpec((None, k, tn), lambda j, i: (l, 0, j))],
        out_specs=pl.BlockSpec((tm, tn), lambda j, i: (i, j)),
        out_shape=jax.ShapeDtypeStruct((t, n), out_dtype),
        compiler_params=_params(("parallel", "parallel"), 48),
        name="proj",
    )(x, w)


def _mm_conv_kernel(x_ref, w_ref, cw_ref, cb_ref, o_ref, carry_ref):
    i = pl.program_id(1)

    @pl.when(i == 0)
    def _():
        carry_ref[...] = jnp.zeros_like(carry_ref)

    raw = _dot(x_ref[...], w_ref[...])
    tm = raw.shape[0]
    prev = carry_ref[...]
    cw = cw_ref[...]
    rows8 = lax.broadcasted_iota(jnp.int32, (8, raw.shape[1]), 0)
    y = cb_ref[...] + cw[CONV_K - 1:CONV_K, :] * raw
    y_head = cb_ref[...] + cw[CONV_K - 1:CONV_K, :] * raw[0:8]
    for s in range(1, CONV_K):
        wk = cw[CONV_K - 1 - s:CONV_K - s, :]
        rolled = pltpu.roll(raw, s, 0)
        y = y + wk * rolled
        head = jnp.where(rows8 < s, pltpu.roll(prev, s, 0), rolled[0:8])
        y_head = y_head + wk * head
    y = jnp.concatenate([y_head, y[8:]], axis=0)
    o_ref[...] = y * _sigmoid(y)
    carry_ref[...] = raw[tm - 8:tm]


def _matmul_conv(x, w, cw, cb, l, tn=1024, tm=1024):
    t, k = x.shape
    n = w.shape[-1]
    return pl.pallas_call(
        _mm_conv_kernel,
        grid=(n // tn, t // tm),
        in_specs=[pl.BlockSpec((tm, k), lambda j, i: (i, 0)),
                  pl.BlockSpec((None, k, tn), lambda j, i: (l, 0, j)),
                  pl.BlockSpec((None, CONV_K, tn), lambda j, i: (l, 0, j)),
                  pl.BlockSpec((None, 1, tn), lambda j, i: (l, 0, j))],
        out_specs=pl.BlockSpec((tm, tn), lambda j, i: (i, j)),
        out_shape=jax.ShapeDtypeStruct((t, n), F32),
        scratch_shapes=[pltpu.VMEM((8, tn), F32)],
        compiler_params=_params(("parallel", "arbitrary"), 48),
        name="proj_conv",
    )(x, w, cw, cb)


def _ssd_kernel(xbc_ref, z_ref, dt_ref, dtb_ref, alog_ref, dskip_ref, ng_ref, o_ref, state_ref):
    c = pl.program_id(0)

    @pl.when(c == 0)
    def _():
        state_ref[...] = jnp.zeros_like(state_ref)

    L = SSD_CHUNK
    P = SSD_HEAD_DIM
    gw = SSD_D_INNER // SSD_GROUPS
    hpg = SSD_HEADS // SSD_GROUPS
    dt = _softplus(dt_ref[...] + dtb_ref[...])
    a = -jnp.exp(alog_ref[...])
    da = dt * a
    r_i = lax.broadcasted_iota(jnp.int32, (L, L), 0)
    c_i = lax.broadcasted_iota(jnp.int32, (L, L), 1)
    tri = r_i >= c_i
    acs = _dot_hi(tri.astype(F32), da)
    acs_t = acs.T
    last = acs[L - 1:L, :]
    e_acs = jnp.exp(acs)
    e_dec = jnp.exp(last - acs)
    e_last = jnp.exp(last)
    lane = lax.broadcasted_iota(jnp.int32, (L, LANE), 1)
    lo_half = lane < P

    def pair_bcast(v, h):
        rows = v.shape[0]
        return jnp.where(lo_half[:rows], jnp.broadcast_to(v[:, h:h + 1], (rows, LANE)),
                         jnp.broadcast_to(v[:, h + 1:h + 2], (rows, LANE)))

    for g in range(SSD_GROUPS):
        bg = xbc_ref[:, SSD_D_INNER + g * SSD_STATE:SSD_D_INNER + (g + 1) * SSD_STATE].astype(BF16)
        cg = xbc_ref[:, SSD_D_INNER + SSD_GROUPS * SSD_STATE + g * SSD_STATE:
                     SSD_D_INNER + SSD_GROUPS * SSD_STATE + (g + 1) * SSD_STATE].astype(BF16)
        cb = _dot_nt(cg, bg)
        s_in = state_ref[g]
        y_off = _dot(cg, s_in.astype(BF16))
        xdec_parts = []
        dec_parts = []
        y_parts = []
        for hp in range(hpg // 2):
            h = g * hpg + 2 * hp
            ch = slice(h * P, (h + 2) * P)
            xs = xbc_ref[:, ch]
            xdt = xs * pair_bcast(dt, h)
            gmats = []
            for hh in (h, h + 1):
                diff = acs[:, hh:hh + 1] - acs_t[hh:hh + 1, :]
                lm = jnp.exp(jnp.where(tri, diff, -jnp.inf))
                gmats.append((lm * cb).astype(BF16))
            lhs = jnp.concatenate(gmats, axis=1)
            rhs = jnp.concatenate([jnp.where(lo_half, xdt, 0.0), jnp.where(lo_half, 0.0, xdt)], axis=0).astype(BF16)
            y = _dot(lhs, rhs)
            y = y + y_off[:, 2 * hp * P:(2 * hp + 2) * P] * pair_bcast(e_acs, h)
            y = y + xs * dskip_ref[:, ch]
            zz = z_ref[:, ch]
            y_parts.append(y * (zz * _sigmoid(zz)))
            xdec_parts.append((xdt * pair_bcast(e_dec, h)).astype(BF16))
            dec_parts.append(pair_bcast(e_last, h))
        xdec = jnp.concatenate(xdec_parts, axis=1)
        dec = jnp.concatenate(dec_parts, axis=1)
        state_ref[g] = s_in * dec + _dot_tn(bg, xdec)
        ssq = sum(jnp.sum(y * y, axis=-1, keepdims=True) for y in y_parts)
        inv_rms = lax.rsqrt(ssq * (1.0 / gw) + EPS)
        for i, y in enumerate(y_parts):
            ch = slice(g * gw + i * LANE, g * gw + (i + 1) * LANE)
            o_ref[:, ch] = (y * inv_rms * ng_ref[:, ch]).astype(o_ref.dtype)


def _ssd(conv, plain, small, dtb, alog, dskip, ng):
    t = conv.shape[0]
    L = SSD_CHUNK
    xbc_blk = GDN_QKV // SSD_XBC
    vec = lambda n: pl.BlockSpec((1, n), lambda c: (0, 0))
    return pl.pallas_call(
        _ssd_kernel,
        grid=(t // L,),
        in_specs=[pl.BlockSpec((L, SSD_XBC), lambda c: (c, xbc_blk)),
                  pl.BlockSpec((L, SSD_D_INNER), lambda c: (c, 0)),
                  pl.BlockSpec((L, LANE), lambda c: (c, 0)),
                  vec(LANE), vec(LANE), vec(SSD_D_INNER), vec(SSD_D_INNER)],
        out_specs=pl.BlockSpec((L, SSD_D_INNER), lambda c: (c, 0)),
        out_shape=jax.ShapeDtypeStruct((t, SSD_D_INNER), BF16),
        scratch_shapes=[pltpu.VMEM((SSD_GROUPS, SSD_STATE, SSD_D_INNER // SSD_GROUPS), F32)],
        compiler_params=_params(("arbitrary",), 32),
        name="ssd",
    )(conv, plain, small, dtb, alog, dskip, ng)


def _gdn_kernel(qkv_ref, z_ref, ae_ref, ao_ref, be_ref, bo_ref, dtb_ref, alog_ref, ng_ref, o_ref, state_ref):
    c = pl.program_id(0)

    @pl.when(c == 0)
    def _():
        state_ref[...] = jnp.zeros_like(state_ref)

    L = GDN_CHUNK
    R = 2 * L
    n_pairs = GDN_HEADS // 2
    row1 = lax.broadcasted_iota(jnp.int32, (R, LANE), 0)
    top = row1 < L
    a2 = jnp.concatenate([ae_ref[...], ao_ref[...]], axis=0)
    b2 = jnp.concatenate([be_ref[...], bo_ref[...]], axis=0)
    dtb2 = jnp.where(top, dtb_ref[0:1, :], dtb_ref[1:2, :])
    alog2 = jnp.where(top, alog_ref[0:1, :], alog_ref[1:2, :])
    beta2 = _sigmoid(b2)
    g2 = -jnp.exp(alog2) * _softplus(a2 + dtb2)
    r_i = lax.broadcasted_iota(jnp.int32, (R, R), 0)
    c_i = lax.broadcasted_iota(jnp.int32, (R, R), 1)
    same = (r_i < L) == (c_i < L)
    incl = same & (r_i >= c_i)
    strict = same & (r_i > c_i)
    gc2 = _dot_hi(incl.astype(F32), g2)
    gc2_t = gc2.T
    gl2 = jnp.where(top, gc2[L - 1:L, :], gc2[R - 1:R, :])
    e_gc = jnp.exp(gc2)
    e_rem = jnp.exp(gl2 - gc2)
    e_gl = jnp.exp(gl2)
    eye = (r_i == c_i).astype(F32)
    merge_masks = []
    for lb in range(int(math.log2(L))):
        r_blk = jnp.right_shift(r_i, lb)
        c_blk = jnp.right_shift(c_i, lb)
        merge_masks.append((r_blk == c_blk + 1) & (jnp.bitwise_and(r_blk, 1) == 1))
    row2 =lax.broadcasted_iota(jnp.int32, (R, 2 * LANE), 0)
    col2 = lax.broadcasted_iota(jnp.int32, (R, 2 * LANE), 1)
    diag2 = (row2 < L) == (col2 < LANE)
    srow = lax.broadcasted_iota(jnp.int32, (2 * GDN_DK, LANE), 0)

    def stack(ref, base, p):
        lo = base + 2 * p * LANE
        return jnp.concatenate([ref[:, lo:lo + LANE], ref[:, lo + LANE:lo + 2 * LANE]], axis=0)

    def blockdiag(x):
        return jnp.where(diag2, jnp.concatenate([x, x], axis=1), 0.0).astype(BF16)

    def col(v, p):
        return jnp.broadcast_to(v[:, p:p + 1], (R, LANE))

    pairs = range(n_pairs)
    a_b, qk_b, rhs_b, q2, k2 = [], [], [], [], []
    for p in pairs:
        q = stack(qkv_ref, 0, p)
        k = stack(qkv_ref, GDN_HEADS * GDN_DK, p)
        v = stack(qkv_ref, 2 * GDN_HEADS * GDN_DK, p)
        qn = q * lax.rsqrt(jnp.sum(q * q, axis=-1, keepdims=True) + EPS) * (GDN_DK ** -0.5)
        kn = k * lax.rsqrt(jnp.sum(k * k, axis=-1, keepdims=True) + EPS)
        beta = col(beta2, p)
        diff = gc2[:, p:p + 1] - gc2_t[p:p + 1, :]
        decay = jnp.exp(jnp.where(incl, diff, -jnp.inf))
        kb = kn * beta
        kn_b = kn.astype(BF16)
        a_b.append(jnp.where(strict, _dot_nt(kb.astype(BF16), kn_b) * decay, 0.0).astype(BF16))
        qk_b.append(jnp.where(incl, _dot_nt(qn.astype(BF16), kn_b) * decay, 0.0).astype(BF16))
        rhs_b.append(jnp.concatenate([v * beta, kb * col(e_gc, p)], axis=1).astype(BF16))
        q2.append(blockdiag(qn * col(e_gc, p)))
        k2.append(blockdiag(kn * col(e_rem, p)))
    zero_b = jnp.zeros((R, R), BF16)
    inv = [eye - jnp.where(merge_masks[0], a_b[p], zero_b).astype(F32) for p in pairs]
    for mk in merge_masks[1:]:
        inv_b = [inv[p].astype(BF16) for p in pairs]
        de = [_dot(inv_b[p], jnp.where(mk, a_b[p], zero_b)).astype(BF16) for p in pairs]
        inv = [inv[p] - _dot(de[p], inv_b[p]) for p in pairs]
    sol = [_dot(inv[p].astype(BF16), rhs_b[p]) for p in pairs]
    s_old = [state_ref[p] for p in pairs]
    s_b = [s_old[p].astype(BF16) for p in pairs]
    v_cb = [(sol[p][:, :GDN_DV] - _dot(blockdiag(sol[p][:, GDN_DV:]), s_b[p])).astype(BF16) for p in pairs]
    o = [_dot(q2[p], s_b[p]) + _dot(qk_b[p], v_cb[p]) for p in pairs]
    for p in pairs:
        sdec = jnp.where(srow < GDN_DK, jnp.broadcast_to(e_gl[0:1, p:p + 1], (2 * GDN_DK, LANE)),
                         jnp.broadcast_to(e_gl[R - 1:R, p:p + 1], (2 * GDN_DK, LANE)))
        state_ref[p] = s_old[p] * sdec + _dot_tn(k2[p], v_cb[p])
    for p in pairs:
        z = stack(z_ref, 0, p)
        on = o[p] * lax.rsqrt(jnp.mean(o[p] * o[p], axis=-1, keepdims=True) + EPS) * ng_ref[...]
        res = (on * (z * _sigmoid(z))).astype(o_ref.dtype)
        o_ref[:, 2 * p * LANE:(2 * p + 1) * LANE] = res[:L]
        o_ref[:, (2 * p + 1) * LANE:(2 * p + 2) * LANE] = res[L:]


def _gdn(conv, plain, small, dtb, alog, ng):
    t = conv.shape[0]
    L = GDN_CHUNK
    dz = GDN_HEADS * GDN_DV
    sm = lambda j: pl.BlockSpec((L, LANE), lambda c: (c, j))
    return pl.pallas_call(
        _gdn_kernel,
        grid=(t // L,),
        in_specs=[pl.BlockSpec((L, GDN_QKV), lambda c: (c, 0)),
                  pl.BlockSpec((L, dz), lambda c: (c, SSD_D_INNER // dz)),
                  sm(1), sm(2), sm(3), sm(4),
                  pl.BlockSpec((2, LANE), lambda c: (0, 0)),
                  pl.BlockSpec((2, LANE), lambda c: (0, 0)),
                  pl.BlockSpec((1, GDN_DV), lambda c: (0, 0))],
        out_specs=pl.BlockSpec((L, dz), lambda c: (c, 0)),
        out_shape=jax.ShapeDtypeStruct((t, dz), BF16),
        scratch_shapes=[pltpu.VMEM((GDN_HEADS // 2, 2 * GDN_DK, GDN_DV), F32)],
        compiler_params=_params(("arbitrary",), 32),
        name="gdn",
    )(conv, plain, small, small, small, small, dtb, alog, ng)


def _attn_kernel(q_ref, k_ref, v_ref, qg_ref, kg_ref, bias_ref, o_ref, kaug_ref, vaug_ref, kmean_ref, s_ref):
    qt = pl.program_id(1)
    B = MOBA_BLOCK
    t = k_ref.shape[0]
    nb = t // B

    @pl.when(qt == 0)
    def _():
        lane_b = lax.broadcasted_iota(jnp.int32, (B, LANE), 1)

        for b in range(nb):
            rows = slice(b * B, (b + 1) * B)
            k = k_ref[rows, :]
            kn = k * lax.rsqrt(jnp.mean(k * k, axis=-1, keepdims=True) + EPS) * kg_ref[...]
            kaug_ref[rows, 0:LANE] = kn.astype(BF16)
            kaug_ref[rows, LANE:2 * LANE] = jnp.where(lane_b == b, 1.0, 0.0).astype(BF16)
            vaug_ref[rows, 0:LANE] = v_ref[rows, :].astype(BF16)
            vaug_ref[rows, LANE:2 * LANE] = jnp.ones((B, LANE), BF16)
            kmean_ref[b:b + 1, :] = jnp.mean(kn, axis=0, keepdims=True)

    q = q_ref[...]
    qn = q * lax.rsqrt(jnp.mean(q * q, axis=-1, keepdims=True) + EPS) * qg_ref[...]
    own = qt
    gate = lax.dot_general(kmean_ref[...], qn, (((1,), (1,)), ((), ())), preferred_element_type=F32,
                           precision=lax.Precision.HIGHEST)
    blk = lax.broadcasted_iota(jnp.int32, gate.shape, 0)
    past = blk < own
    gate = jnp.where(past, gate, NEG_INF)
    chosen = blk == own
    blk_f = blk.astype(F32)
    for _ in range(MOBA_TOPK):
        best = jnp.max(gate, axis=0, keepdims=True)
        first = jnp.min(jnp.where(gate == best, blk_f, float(nb)), axis=0, keepdims=True)
        hit = blk_f == first
        chosen = chosen | (hit & past)
        gate = jnp.where(hit, -jnp.inf, gate)
    selb_t = jnp.where(chosen, 0.0, NEG_INF)
    selb_t = jnp.concatenate([selb_t, jnp.zeros((LANE - nb, B), F32)], axis=0)
    selb = selb_t.T
    q_aug = jnp.concatenate([(qn * (ATTN_HEAD_DIM ** -0.5 * LOG2E)).astype(BF16), selb.astype(BF16)], axis=1)

    def attend(n_blocks):
        m = jnp.full((B, 1), NEG_INF, F32)
        for j in range(n_blocks):
            tile = jnp.where(j > own, N_BIAS_TILES - 1, jnp.minimum(own - j, N_BIAS_TILES - 2))
            s = _dot_nt(q_aug, kaug_ref[j * B:(j + 1) * B, :]) + bias_ref[tile]
            s_ref[j] = s
            m = jnp.maximum(m, jnp.max(s, axis=-1, keepdims=True))
        acc = jnp.zeros((B, 2 * LANE), F32)
        for j in range(n_blocks):
            p = jnp.exp2(s_ref[j] - m)
            acc = acc + _dot(p.astype(BF16), vaug_ref[j * B:(j + 1) * B, :])
        o_ref[...] = (acc[:, :ATTN_HEAD_DIM] / acc[:, ATTN_HEAD_DIM:ATTN_HEAD_DIM + 1]).astype(o_ref.dtype)

    n_branches = -(-nb // ATTN_UNROLL)
    for i in range(n_branches):
        @pl.when(own // ATTN_UNROLL == i)
        def _(i=i):
            attend(min((i + 1) * ATTN_UNROLL, nb))


def _attn(plain, qg, kg, bias_tiles):
    t = plain.shape[0]
    B = MOBA_BLOCK
    hd = ATTN_HEAD_DIM
    base = (SSD_D_INNER + GDN_HEADS * GDN_DV) // hd
    return pl.pallas_call(
        _attn_kernel,
        grid=(ATTN_HEADS, t // B),
        in_specs=[pl.BlockSpec((B, hd), lambda h, i: (i, base + h)),
                  pl.BlockSpec((t, hd), lambda h, i: (0, base + ATTN_HEADS + h)),
                  pl.BlockSpec((t, hd), lambda h, i: (0, base + 2 * ATTN_HEADS + h)),
                  pl.BlockSpec((1, hd), lambda h, i: (0, 0)),
                  pl.BlockSpec((1, hd), lambda h, i: (0, 0)),
                  pl.BlockSpec((None, N_BIAS_TILES, B, B), lambda h, i: (h, 0, 0, 0))],
        out_specs=pl.BlockSpec((B, hd), lambda h, i: (i, h)),
        out_shape=jax.ShapeDtypeStruct((t, ATTN_HEADS * hd), BF16),
        scratch_shapes=[pltpu.VMEM((t, 2 * LANE), BF16), pltpu.VMEM((t, 2 * LANE), BF16),
                        pltpu.VMEM((t // B, hd), F32), pltpu.VMEM((t // B, B, B), F32)],
        compiler_params=_params(("arbitrary", "arbitrary"), 56),
        name="moba",
    )(plain, plain, plain, qg, kg, bias_tiles)


def _bias_tiles(rel_bias, t):
    B = MOBA_BLOCK
    dist = jnp.arange(t, dtype=jnp.int32)
    exact = REL_BUCKETS // 2
    nf = jnp.maximum(dist, 1).astype(F32)
    large = exact + (jnp.log(nf / exact) / math.log(REL_MAX_DIST / exact) * (REL_BUCKETS - exact)).astype(jnp.int32)
    bucket = jnp.where(dist < exact, dist, jnp.minimum(large, REL_BUCKETS - 1))
    by_dist = rel_bias.T[:, bucket] * LOG2E
    nh = by_dist.shape[0]
    i = jnp.arange(B)[:, None]
    j = jnp.arange(B)[None, :]
    tiles = []
    for d in range(N_BIAS_TILES - 2):
        idx = jnp.clip(d * B + jnp.arange(2 * B - 1) - (B - 1), 0, t - 1)
        rev = jnp.pad(by_dist[:, idx][:, ::-1], ((0, 0), (0, 1)))
        skew = jnp.broadcast_to(rev[:, None, :], (nh, B, 2 * B)).reshape(nh, 2 * B * B)
        tile = skew[:, :B * (2 * B - 1)].reshape(nh, B, 2 * B - 1)[:, :, B - 1:]
        if d == 0:
            tile = jnp.where(i >= j, tile, NEG_INF)
        tiles.append(tile)
    far = jnp.broadcast_to((rel_bias.T[:, REL_BUCKETS - 1] * LOG2E)[:, None, None], tiles[0].shape)
    tiles.append(far)
    tiles.append(jnp.full_like(far, NEG_INF))
    return jnp.stack(tiles, axis=1)


def _merge_kernel(y1_ref, y2_ref, y3_ref, w1_ref, w2_ref, w3_ref, g1_ref, g2_ref, g3_ref, o_ref):
    acc = _sigmoid(g1_ref[...]) * _dot(y1_ref[...], w1_ref[...])
    acc = acc + _sigmoid(g2_ref[...]) * _dot(y2_ref[...], w2_ref[...])
    acc = acc + _sigmoid(g3_ref[...]) * _dot(y3_ref[...], w3_ref[...])
    o_ref[...] = acc.astype(o_ref.dtype)


def _merge(y_ssd, y_gdn, y_attn, w1, w2, w3, plain, l):
    t, d = y_ssd.shape
    tm, tn = 512, 512
    gbase = (SSD_D_INNER + GDN_HEADS * GDN_DV + ATTN_QKV) // tn
    yspec = pl.BlockSpec((tm, d), lambda i, j: (i, 0))
    wspec = pl.BlockSpec((None, d, tn), lambda i, j: (l, 0, j))
    gspec = lambda b: pl.BlockSpec((tm, tn), lambda i, j: (i, gbase + b * (D_MODEL // tn) + j))
    return pl.pallas_call(
        _merge_kernel,
        grid=(t // tm, D_MODEL // tn),
        in_specs=[yspec, yspec, yspec, wspec, wspec, wspec, gspec(0), gspec(1), gspec(2)],
        out_specs=pl.BlockSpec((tm, tn), lambda i, j: (i, j)),
        out_shape=jax.ShapeDtypeStruct((t, D_MODEL), BF16),
        compiler_params=_params(("parallel", "parallel"), 48),
        name="merge",
    )(y_ssd, y_gdn, y_attn, w1, w2, w3, plain, plain, plain)


def _out_kernel(x_ref, w_ref, h_ref, g_ref, o_ref):
    o_ref[...] = h_ref[...] + g_ref[...] * _dot(x_ref[...], w_ref[...])


def _out_proj(x, w, h, g, l):
    t, k = x.shape
    n = w.shape[-1]
    tm, tn = 1024, 1024
    return pl.pallas_call(
        _out_kernel,
        grid=(n // tn, t // tm),
        in_specs=[pl.BlockSpec((tm, k), lambda j, i: (i, 0)),
                  pl.BlockSpec((None, k, tn), lambda j, i: (l, 0, j)),
                  pl.BlockSpec((tm, tn), lambda j, i: (i, j)),
                  pl.BlockSpec((1, tn), lambda j, i: (0, j))],
        out_specs=pl.BlockSpec((tm, tn), lambda j, i: (i, j)),
        out_shape=jax.ShapeDtypeStruct((t, n), F32),
        compiler_params=_params(("parallel", "parallel"), 48),
        name="out_proj",
    )(x, w, h, g)


def _pad_lanes(w, n=LANE):
    return jnp.pad(w, [(0, 0)] * (w.ndim - 1) + [(0, n - w.shape[-1])])


def _split_in(w_in):
    cuts = []
    lo = 0
    for s in IN_SPLITS:
        cuts.append(w_in[..., lo:lo + s])
        lo += s
    return cuts


def kernel(x, c, w_mod, b_mod, norm_ffn1, ffn1_w_gate, ffn1_w_up, ffn1_w_down, norm_mix, w_in, ssd_conv_w, ssd_conv_b, ssd_dt_bias, ssd_a_log, ssd_d, ssd_norm, w_o_ssd, gdn_conv_w, gdn_dt_bias, gdn_a_log, gdn_norm, w_o_gdn, attn_q_norm, attn_k_norm, rel_bias, w_o_attn, w_out, norm_ffn2, ffn2_w_gate, ffn2_w_up, ffn2_w_down):
    bsz, t, d = x.shape
    assert bsz == 1 and d == D_MODEL and t % 1024 == 0 and t // MOBA_BLOCK <= LANE
    depth = w_mod.shape[0]

    ssd_z, ssd_xbc, ssd_dt, gdn_qkv, gdn_z, gdn_a, gdn_b, attn_qkv, gates = _split_in(w_in)
    w_conv = jnp.concatenate([gdn_qkv, ssd_xbc], axis=-1).astype(BF16)
    w_plain = jnp.concatenate([ssd_z, gdn_z, attn_qkv, gates], axis=-1).astype(BF16)
    w_small = jnp.concatenate([_pad_lanes(ssd_dt), _pad_lanes(gdn_a[..., 0::2]), _pad_lanes(gdn_a[..., 1::2]),
                               _pad_lanes(gdn_b[..., 0::2]), _pad_lanes(gdn_b[..., 1::2])], axis=-1).astype(BF16)
    conv_w = jnp.concatenate([gdn_conv_w, ssd_conv_w], axis=-1)
    conv_b = jnp.concatenate([jnp.zeros((depth, GDN_QKV), F32), ssd_conv_b], axis=-1).reshape(depth, 1, CONV_N)
    ssd_dtb = _pad_lanes(ssd_dt_bias)
    ssd_alog = _pad_lanes(ssd_a_log)
    ssd_dskip = jnp.repeat(ssd_d, SSD_HEAD_DIM, axis=-1)
    gdn_dtb = jnp.stack([_pad_lanes(gdn_dt_bias[:, 0::2]), _pad_lanes(gdn_dt_bias[:, 1::2])], axis=1)
    gdn_alog = jnp.stack([_pad_lanes(gdn_a_log[:, 0::2]), _pad_lanes(gdn_a_log[:, 1::2])], axis=1)
    f1g, f1u, f1d = ffn1_w_gate.astype(BF16), ffn1_w_up.astype(BF16), ffn1_w_down.astype(BF16)
    f2g, f2u, f2d = ffn2_w_gate.astype(BF16), ffn2_w_up.astype(BF16), ffn2_w_down.astype(BF16)
    wo_ssd, wo_gdn, wo_attn, wo = (w_o_ssd.astype(BF16), w_o_gdn.astype(BF16), w_o_attn.astype(BF16),
                                   w_out.astype(BF16))
    bias_tiles = _bias_tiles(rel_bias, t)

    mod = _mod_all(c, w_mod, b_mod).reshape(depth, N_MOD, d)
    h = x.reshape(t, d)
    for l in range(depth):
        sh1, sc1, g1, sh2, sc2, g2, sh3, sc3, g3 = [mod[l, i][None, :] for i in range(N_MOD)]
        h = _ffn(h, norm_ffn1[l][None, :], sh1, sc1, g1, f1g, f1u, f1d, l)
        u = _modulate(h, norm_mix[l][None, :], sh2, sc2)
        conv = _matmul_conv(u, w_conv, conv_w, conv_b, l)
        plain = _matmul(u, w_plain, l, tn=1024)
        small = _matmul(u, w_small, l, tn=SMALL_N)
        y_ssd = _ssd(conv, plain, small, ssd_dtb[l][None, :], ssd_alog[l][None, :], ssd_dskip[l][None, :],
                     ssd_norm[l][None, :])
        y_gdn = _gdn(conv, plain, small, gdn_dtb[l], gdn_alog[l], gdn_norm[l][None, :])
        y_attn = _attn(plain, attn_q_norm[l][None, :], attn_k_norm[l][None, :], bias_tiles)
        merged = _merge(y_ssd, y_gdn, y_attn, wo_ssd, wo_gdn, wo_attn, plain, l)
        h = _out_proj(merged, wo, h, g2, l)
        h = _ffn(h, norm_ffn2[l][None, :], sh3, sc3, g3, f2g, f2u, f2d, l)
    return h.reshape(bsz, t, d)
```

```python
import functools
import math

import jax
import jax.numpy as jnp
from jax import lax
from jax.experimental import pallas as pl
from jax.experimental.pallas import tpu as pltpu

F32 = jnp.float32
BF16 = jnp.bfloat16

D_MODEL = 2048
DEPTH = 4
D_FF = 5632
CONV_K = 4
SSD_HEADS = 32
SSD_HEAD_DIM = 64
SSD_D_INNER = SSD_HEADS * SSD_HEAD_DIM
SSD_GROUPS = 4
SSD_STATE = 128
SSD_CHUNK = 128
GDN_HEADS = 16
GDN_DK = 128
GDN_DV = 128
GDN_CHUNK = 64
ATTN_HEADS = 16
ATTN_HEAD_DIM = 128
MOBA_BLOCK = 256
MOBA_TOPK = 3
REL_BUCKETS = 32
REL_MAX_DIST = 4096
N_BRANCHES = 3
N_MOD = 9
EPS = 1e-6
NEG_INF = -1e30

SSD_XBC = SSD_D_INNER + 2 * SSD_GROUPS * SSD_STATE
GDN_QKV = GDN_HEADS * (2 * GDN_DK + GDN_DV)
ATTN_QKV = 3 * ATTN_HEADS * ATTN_HEAD_DIM
IN_SPLITS = (SSD_D_INNER, SSD_XBC, SSD_HEADS, GDN_QKV, GDN_HEADS * GDN_DV, GDN_HEADS, GDN_HEADS, ATTN_QKV,
             N_BRANCHES * D_MODEL)

LANE = 128
V7X_VMEM_BYTES = 64 * 1024 * 1024
N_BIAS_TILES = 15
ATTN_UNROLL = 8
ATTN_QK_BLOCKS = 4
ATTN_VT_ROWS = ATTN_HEAD_DIM + 16
LOG2E = 1.4426950408889634
CONV_COL_CHUNK = 256

CONV_N = GDN_QKV + SSD_XBC
PLAIN_N = SSD_D_INNER + GDN_HEADS * GDN_DV + ATTN_QKV + N_BRANCHES * D_MODEL
SMALL_N = 5 * LANE


def _params(semantics, vmem_mb):
    return pltpu.CompilerParams(dimension_semantics=semantics, vmem_limit_bytes=vmem_mb * 1024 * 1024)


def _sigmoid(x):
    return 1.0 / (1.0 + jnp.exp(-x))


def _softplus(x):
    return jnp.maximum(x, 0.0) + jnp.log1p(jnp.exp(-jnp.abs(x)))


def _dot(a, b):
    return jnp.dot(a, b, preferred_element_type=F32)


def _dot_nt(a, b):
    return lax.dot_general(a, b, (((1,), (1,)), ((), ())), preferred_element_type=F32)


def _dot_tn(a, b):
    return lax.dot_general(a, b, (((0,), (0,)), ((), ())), preferred_element_type=F32)


def _dot_hi(a, b):
    return jnp.dot(a, b, preferred_element_type=F32, precision=lax.Precision.HIGHEST)


def _mod_kernel(cb_ref, w_ref, b_ref, o_ref):
    cb = cb_ref[...]
    tn = w_ref.shape[-1]
    for j in range(tn // LANE):
        sl = slice(j * LANE, (j + 1) * LANE)
        o_ref[:, sl] = jnp.sum(w_ref[:, sl] * cb, axis=0, keepdims=True) + b_ref[:, sl]


def _mod_all(c, w_mod, b_mod):
    depth, d, n = w_mod.shape
    tn = 1024
    cb = jnp.broadcast_to(c.reshape(d, 1), (d, LANE))
    return pl.pallas_call(
        _mod_kernel,
        grid=(depth, n // tn),
        in_specs=[pl.BlockSpec((d, LANE), lambda l, j: (0, 0)),
                  pl.BlockSpec((None, d, tn), lambda l, j: (l, 0, j)),
                  pl.BlockSpec((None, 1, tn), lambda l, j: (l, 0, j))],
        out_specs=pl.BlockSpec((None, 1, tn), lambda l, j: (l, 0, j)),
        out_shape=jax.ShapeDtypeStruct((depth, 1, n), F32),
        compiler_params=_params(("parallel", "parallel"), 40),
        name="adaln_mod",
    )(cb, w_mod, b_mod.reshape(depth, 1, n))


def _modulated(x, gn, sh, sc):
    xn = x * lax.rsqrt(jnp.mean(x * x, axis=-1, keepdims=True) + EPS)
    return (xn * gn) * (1.0 + sc) + sh


def _modulate_kernel(h_ref, gn_ref, sh_ref, sc_ref, o_ref):
    o_ref[...] = _modulated(h_ref[...], gn_ref[...], sh_ref[...], sc_ref[...]).astype(o_ref.dtype)


def _modulate(h, gn, sh, sc):
    t, d = h.shape
    tm = 512
    vec = pl.BlockSpec((1, d), lambda i: (0, 0))
    return pl.pallas_call(
        _modulate_kernel,
        grid=(t // tm,),
        in_specs=[pl.BlockSpec((tm, d), lambda i: (i, 0)), vec, vec, vec],
        out_specs=pl.BlockSpec((tm, d), lambda i: (i, 0)),
        out_shape=jax.ShapeDtypeStruct((t, d), BF16),
        compiler_params=_params(("parallel",), 32),
        name="modulate",
    )(h, gn, sh, sc)


def _ffn_kernel(h_ref, gn_ref, sh_ref, sc_ref, g_ref, wg_ref, wu_ref, wd_ref, o_ref, xm_ref):
    f = pl.program_id(1)

    @pl.when(f == 0)
    def _():
        xm_ref[...] = _modulated(h_ref[...], gn_ref[...], sh_ref[...], sc_ref[...]).astype(BF16)
        o_ref[...] = jnp.zeros_like(o_ref)

    xm = xm_ref[...]
    a = _dot(xm, wg_ref[...])
    b = _dot(xm, wu_ref[...])
    hm = ((a * _sigmoid(a)) * b).astype(BF16)
    o_ref[...] += _dot(hm, wd_ref[...])

    @pl.when(f == pl.num_programs(1) - 1)
    def _():
        o_ref[...] = h_ref[...] + (0.5 * g_ref[...]) * o_ref[...]


def _ffn(h, gn, sh, sc, g, wg, wu, wd, l):
    t, d = h.shape
    ff = wg.shape[-1]
    tm, tf = 512, 512
    vec = pl.BlockSpec((1, d), lambda i, f: (0, 0))
    return pl.pallas_call(
        _ffn_kernel,
        grid=(t // tm, ff // tf),
        in_specs=[pl.BlockSpec((tm, d), lambda i, f: (i, 0)), vec, vec, vec, vec,
                  pl.BlockSpec((None, d, tf), lambda i, f: (l, 0, f)),
                  pl.BlockSpec((None, d, tf), lambda i, f: (l, 0, f)),
                  pl.BlockSpec((None, tf, d), lambda i, f: (l, f, 0))],
        out_specs=pl.BlockSpec((tm, d), lambda i, f: (i, 0)),
        out_shape=jax.ShapeDtypeStruct((t, d), F32),
        scratch_shapes=[pltpu.VMEM((tm, d), BF16)],
        compiler_params=_params(("parallel", "arbitrary"), 48),
        name="ffn",
    )(h, gn, sh, sc, g, wg, wu, wd)


def _mm_kernel(x_ref, w_ref, o_ref):
    o_ref[...] = _dot(x_ref[...], w_ref[...]).astype(o_ref.dtype)


def _matmul(x, w, l, tn, tm=1024, out_dtype=F32):
    t, k = x.shape
    n = w.shape[-1]
    return pl.pallas_call(
        _mm_kernel,
        grid=(n // tn, t // tm),
        in_specs=[pl.BlockSpec((tm, k), lambda j, i: (i, 0)),
                  pl.BlockSpec((None, k, tn), lambda j, i: (l, 0, j))],
        out_specs=pl.BlockSpec((tm, tn), lambda j, i: (i, j)),
        out_shape=jax.ShapeDtypeStruct((t, n), out_dtype),
        compiler_params=_params(("parallel", "parallel"), 48),
        name="proj",
    )(x, w)


def _mm_conv_kernel(x_ref, w_ref, cw_ref, cb_ref, o_ref, win_ref):
    i = pl.program_id(1)

    @pl.when(i == 0)
    def _():
        win_ref[:, 0:8, :] = jnp.zeros((win_ref.shape[0], 8, win_ref.shape[2]), F32)

    tm, tn = o_ref.shape
    cc = CONV_COL_CHUNK
    n_chunks = tn // cc

    def project(c):
        win_ref[c, 8:8 + tm, :] = _dot(x_ref[...], w_ref[:, c * cc:(c + 1) * cc])

    def conv(c):
        cs = slice(c * cc, (c + 1) * cc)
        cw = cw_ref[:, cs]
        y = cb_ref[:, cs] + cw[CONV_K - 1:CONV_K, :] * win_ref[c, 8:8 + tm, :]
        for s in range(1, CONV_K):
            y = y + cw[CONV_K - 1 - s:CONV_K - s, :] * win_ref[c, 8 - s:8 - s + tm, :]
        o_ref[:, cs] = y * _sigmoid(y)
        win_ref[c, 0:8, :] = win_ref[c, tm:tm + 8, :]

    project(0)
    for c in range(1, n_chunks):
        project(c)
        conv(c - 1)
    conv(n_chunks - 1)


def _matmul_conv(x, w, cw, cb, l, tn=1024, tm=1024):
    t, k = x.shape
    n = w.shape[-1]
    return pl.pallas_call(
        _mm_conv_kernel,
        grid=(n // tn, t // tm),
        in_specs=[pl.BlockSpec((tm, k), lambda j, i: (i, 0)),
                  pl.BlockSpec((None, k, tn), lambda j, i: (l, 0, j)),
                  pl.BlockSpec((None, CONV_K, tn), lambda j, i: (l, 0, j)),
                  pl.BlockSpec((None, 1, tn), lambda j, i: (l, 0, j))],
        out_specs=pl.BlockSpec((tm, tn), lambda j, i: (i, j)),
        out_shape=jax.ShapeDtypeStruct((t, n), F32),
        scratch_shapes=[pltpu.VMEM((tn // CONV_COL_CHUNK, tm + 8, CONV_COL_CHUNK), F32)],
        compiler_params=_params(("parallel", "arbitrary"), 48),
        name="proj_conv",
    )(x, w, cw, cb)


def _ssd_kernel(xbc_ref, z_ref, dt_ref, dtb_ref, alog_ref, dskip_ref, ng_ref, o_ref, state_ref):
    c = pl.program_id(0)

    @pl.when(c == 0)
    def _():
        state_ref[...] = jnp.zeros_like(state_ref)

    L = SSD_CHUNK
    P = SSD_HEAD_DIM
    gw = SSD_D_INNER // SSD_GROUPS
    hpg = SSD_HEADS // SSD_GROUPS
    dt = _softplus(dt_ref[...] + dtb_ref[...])
    a = -jnp.exp(alog_ref[...])
    da = dt * a
    r_i = lax.broadcasted_iota(jnp.int32, (L, L), 0)
    c_i = lax.broadcasted_iota(jnp.int32, (L, L), 1)
    tri = r_i >= c_i
    acs = _dot_hi(tri.astype(F32), da)
    acs_t = acs.T
    last = acs[L - 1:L, :]
    e_acs = jnp.exp(acs)
    e_dec = jnp.exp(last - acs)
    e_last = jnp.exp(last)
    lane = lax.broadcasted_iota(jnp.int32, (L, LANE), 1)
    lo_half = lane < P

    def pair_bcast(v, h):
        rows = v.shape[0]
        return jnp.where(lo_half[:rows], jnp.broadcast_to(v[:, h:h + 1], (rows, LANE)),
                         jnp.broadcast_to(v[:, h + 1:h + 2], (rows, LANE)))

    for g in range(SSD_GROUPS):
        bg = xbc_ref[:, SSD_D_INNER + g * SSD_STATE:SSD_D_INNER + (g + 1) * SSD_STATE].astype(BF16)
        cg = xbc_ref[:, SSD_D_INNER + SSD_GROUPS * SSD_STATE + g * SSD_STATE:
                     SSD_D_INNER + SSD_GROUPS * SSD_STATE + (g + 1) * SSD_STATE].astype(BF16)
        cb = _dot_nt(cg, bg)
        s_in = state_ref[g]
        y_off = _dot(cg, s_in.astype(BF16))
        xdec_parts = []
        dec_parts = []
        y_parts = []
        for hp in range(hpg // 2):
            h = g * hpg + 2 * hp
            ch = slice(h * P, (h + 2) * P)
            xs = xbc_ref[:, ch]
            xdt = xs * pair_bcast(dt, h)
            gmats = []
            for hh in (h, h + 1):
                diff = acs[:, hh:hh + 1] - acs_t[hh:hh + 1, :]
                lm = jnp.exp(jnp.where(tri, diff, -jnp.inf))
                gmats.append((lm * cb).astype(BF16))
            lhs = jnp.concatenate(gmats, axis=1)
            rhs = jnp.concatenate([jnp.where(lo_half, xdt, 0.0), jnp.where(lo_half, 0.0, xdt)], axis=0).astype(BF16)
            y = _dot(lhs, rhs)
            y = y + y_off[:, 2 * hp * P:(2 * hp + 2) * P] * pair_bcast(e_acs, h)
            y = y + xs * dskip_ref[:, ch]
            zz = z_ref[:, ch]
            y_parts.append(y * (zz * _sigmoid(zz)))
            xdec_parts.append((xdt * pair_bcast(e_dec, h)).astype(BF16))
            dec_parts.append(pair_bcast(e_last, h))
        xdec = jnp.concatenate(xdec_parts, axis=1)
        dec = jnp.concatenate(dec_parts, axis=1)
        state_ref[g] = s_in * dec + _dot_tn(bg, xdec)
        ssq = sum(jnp.sum(y * y, axis=-1, keepdims=True) for y in y_parts)
        inv_rms = lax.rsqrt(ssq * (1.0 / gw) + EPS)
        for i, y in enumerate(y_parts):
            ch = slice(g * gw + i * LANE, g * gw + (i + 1) * LANE)
            o_ref[:, ch] = (y * inv_rms * ng_ref[:, ch]).astype(o_ref.dtype)


def _ssd(conv, plain, small, dtb, alog, dskip, ng):
    t = conv.shape[0]
    L = SSD_CHUNK
    xbc_blk = GDN_QKV // SSD_XBC
    vec = lambda n: pl.BlockSpec((1, n), lambda c: (0, 0))
    return pl.pallas_call(
        _ssd_kernel,
        grid=(t // L,),
        in_specs=[pl.BlockSpec((L, SSD_XBC), lambda c: (c, xbc_blk)),
                  pl.BlockSpec((L, SSD_D_INNER), lambda c: (c, 0)),
                  pl.BlockSpec((L, LANE), lambda c: (c, 0)),
                  vec(LANE), vec(LANE), vec(SSD_D_INNER), vec(SSD_D_INNER)],
        out_specs=pl.BlockSpec((L, SSD_D_INNER), lambda c: (c, 0)),
        out_shape=jax.ShapeDtypeStruct((t, SSD_D_INNER), BF16),
        scratch_shapes=[pltpu.VMEM((SSD_GROUPS, SSD_STATE, SSD_D_INNER // SSD_GROUPS), F32)],
        compiler_params=_params(("arbitrary",), 32),
        name="ssd",
    )(conv, plain, small, dtb, alog, dskip, ng)


def _gdn_kernel(qkv_ref, z_ref, ae_ref, ao_ref, be_ref, bo_ref, dtb_ref, alog_ref, ng_ref, o_ref, state_ref):
    c = pl.program_id(0)

    @pl.when(c == 0)
    def _():
        state_ref[...] = jnp.zeros_like(state_ref)

    L = GDN_CHUNK
    R = 2 * L
    n_pairs = GDN_HEADS // 2
    row1 = lax.broadcasted_iota(jnp.int32, (R, LANE), 0)
    top = row1 < L
    a2 = jnp.concatenate([ae_ref[...], ao_ref[...]], axis=0)
    b2 = jnp.concatenate([be_ref[...], bo_ref[...]], axis=0)
    dtb2 = jnp.where(top, dtb_ref[0:1, :], dtb_ref[1:2, :])
    alog2 = jnp.where(top, alog_ref[0:1, :], alog_ref[1:2, :])
    beta2 = _sigmoid(b2)
    g2 = -jnp.exp(alog2) * _softplus(a2 + dtb2)
    r_i = lax.broadcasted_iota(jnp.int32, (R, R), 0)
    c_i = lax.broadcasted_iota(jnp.int32, (R, R), 1)
    same = (r_i < L) == (c_i < L)
    incl = same & (r_i >= c_i)
    strict = same & (r_i > c_i)
    gc2 = _dot_hi(incl.astype(F32), g2)
    gc2_t = gc2.T
    gl2 = jnp.where(top, gc2[L - 1:L, :], gc2[R - 1:R, :])
    e_gc = jnp.exp(gc2)
    e_rem = jnp.exp(gl2 - gc2)
    e_gl = jnp.exp(gl2)
    eye = (r_i == c_i).astype(F32)
    merge_masks = []
    for lb in range(int(math.log2(L))):
        r_blk = jnp.right_shift(r_i, lb)
        c_blk = jnp.right_shift(c_i, lb)
        merge_masks.append((r_blk == c_blk + 1) & (jnp.bitwise_and(r_blk, 1) == 1))
    row2 =lax.broadcasted_iota(jnp.int32, (R, 2 * LANE), 0)
    col2 = lax.broadcasted_iota(jnp.int32, (R, 2 * LANE), 1)
    diag2 = (row2 < L) == (col2 < LANE)
    srow = lax.broadcasted_iota(jnp.int32, (2 * GDN_DK, LANE), 0)

    def stack(ref, base, p):
        lo = base + 2 * p * LANE
        return jnp.concatenate([ref[:, lo:lo + LANE], ref[:, lo + LANE:lo + 2 * LANE]], axis=0)

    def blockdiag(x):
        return jnp.where(diag2, jnp.concatenate([x, x], axis=1), 0.0).astype(BF16)

    def col(v, p):
        return jnp.broadcast_to(v[:, p:p + 1], (R, LANE))

    pairs = range(n_pairs)
    a_b, qk_b, rhs_b, q2, k2 = [], [], [], [], []
    for p in pairs:
        q = stack(qkv_ref, 0, p)
        k = stack(qkv_ref, GDN_HEADS * GDN_DK, p)
        v = stack(qkv_ref, 2 * GDN_HEADS * GDN_DK, p)
        qn = q * lax.rsqrt(jnp.sum(q * q, axis=-1, keepdims=True) + EPS) * (GDN_DK ** -0.5)
        kn = k * lax.rsqrt(jnp.sum(k * k, axis=-1, keepdims=True) + EPS)
        beta = col(beta2, p)
        diff = gc2[:, p:p + 1] - gc2_t[p:p + 1, :]
        decay = jnp.exp(jnp.where(incl, diff, -jnp.inf))
        kb = kn * beta
        kn_b = kn.astype(BF16)
        a_b.append(jnp.where(strict, _dot_nt(kb.astype(BF16), kn_b) * decay, 0.0).astype(BF16))
        qk_b.append(jnp.where(incl, _dot_nt(qn.astype(BF16), kn_b) * decay, 0.0).astype(BF16))
        rhs_b.append(jnp.concatenate([v * beta, kb * col(e_gc, p)], axis=1).astype(BF16))
        q2.append(blockdiag(qn * col(e_gc, p)))
        k2.append(blockdiag(kn * col(e_rem, p)))
    zero_b = jnp.zeros((R, R), BF16)
    inv = [eye - jnp.where(merge_masks[0], a_b[p], zero_b).astype(F32) for p in pairs]
    for mk in merge_masks[1:]:
        inv_b = [inv[p].astype(BF16) for p in pairs]
        de = [_dot(inv_b[p], jnp.where(mk, a_b[p], zero_b)).astype(BF16) for p in pairs]
        inv = [inv[p] - _dot(de[p], inv_b[p]) for p in pairs]
    sol = [_dot(inv[p].astype(BF16), rhs_b[p]) for p in pairs]
    s_old = [state_ref[p] for p in pairs]
    s_b = [s_old[p].astype(BF16) for p in pairs]
    v_cb = [(sol[p][:, :GDN_DV] - _dot(blockdiag(sol[p][:, GDN_DV:]), s_b[p])).astype(BF16) for p in pairs]
    o = [_dot(q2[p], s_b[p]) + _dot(qk_b[p], v_cb[p]) for p in pairs]
    for p in pairs:
        sdec = jnp.where(srow < GDN_DK, jnp.broadcast_to(e_gl[0:1, p:p + 1], (2 * GDN_DK, LANE)),
                         jnp.broadcast_to(e_gl[R - 1:R, p:p + 1], (2 * GDN_DK, LANE)))
        state_ref[p] = s_old[p] * sdec + _dot_tn(k2[p], v_cb[p])
    for p in pairs:
        z = stack(z_ref, 0, p)
        on = o[p] * lax.rsqrt(jnp.mean(o[p] * o[p], axis=-1, keepdims=True) + EPS) * ng_ref[...]
        res = (on * (z * _sigmoid(z))).astype(o_ref.dtype)
        o_ref[:, 2 * p * LANE:(2 * p + 1) * LANE] = res[:L]
        o_ref[:, (2 * p + 1) * LANE:(2 * p + 2) * LANE] = res[L:]


def _gdn(conv, plain, small, dtb, alog, ng):
    t = conv.shape[0]
    L = GDN_CHUNK
    dz = GDN_HEADS * GDN_DV
    sm = lambda j: pl.BlockSpec((L, LANE), lambda c: (c, j))
    return pl.pallas_call(
        _gdn_kernel,
        grid=(t // L,),
        in_specs=[pl.BlockSpec((L, GDN_QKV), lambda c: (c, 0)),
                  pl.BlockSpec((L, dz), lambda c: (c, SSD_D_INNER // dz)),
                  sm(1), sm(2), sm(3), sm(4),
                  pl.BlockSpec((2, LANE), lambda c: (0, 0)),
                  pl.BlockSpec((2, LANE), lambda c: (0, 0)),
                  pl.BlockSpec((1, GDN_DV), lambda c: (0, 0))],
        out_specs=pl.BlockSpec((L, dz), lambda c: (c, 0)),
        out_shape=jax.ShapeDtypeStruct((t, dz), BF16),
        scratch_shapes=[pltpu.VMEM((GDN_HEADS // 2, 2 * GDN_DK, GDN_DV), F32)],
        compiler_params=_params(("arbitrary",), 32),
        name="gdn",
    )(conv, plain, small, small, small, small, dtb, alog, ng)


def _attn_kernel(q_ref, k_ref, v_ref, qg_ref, kg_ref, bias_ref, o_ref, kaug_ref, vaug_ref, kmean_ref, s_ref):
    qt = pl.program_id(1)
    B = MOBA_BLOCK
    t = k_ref.shape[0]
    nb = t // B

    @pl.when(qt == 0)
    def _():
        lane_b = lax.broadcasted_iota(jnp.int32, (B, LANE), 1)
        vaug_ref[ATTN_HEAD_DIM:ATTN_VT_ROWS, :] = jnp.ones((ATTN_VT_ROWS - ATTN_HEAD_DIM, t), BF16)
        for b in range(nb):
            rows = slice(b * B, (b + 1) * B)
            k = k_ref[rows, :]
            kn = k * lax.rsqrt(jnp.mean(k * k, axis=-1, keepdims=True) + EPS) * kg_ref[...]
            kaug_ref[rows, 0:LANE] = kn.astype(BF16)
            kaug_ref[rows, LANE:2 * LANE] = jnp.where(lane_b == b, 1.0, 0.0).astype(BF16)
            vaug_ref[0:ATTN_HEAD_DIM, rows] = v_ref[rows, :].T.astype(BF16)
            kmean_ref[b:b + 1, :] = jnp.mean(kn, axis=0, keepdims=True)

    q = q_ref[...]
    qn = q * lax.rsqrt(jnp.mean(q * q, axis=-1, keepdims=True) + EPS) * qg_ref[...]
    own = qt
    gate = lax.dot_general(kmean_ref[...], qn, (((1,), (1,)), ((), ())), preferred_element_type=F32,
                           precision=lax.Precision.HIGHEST)
    blk = lax.broadcasted_iota(jnp.int32, gate.shape, 0)
    past = blk < own
    gate = jnp.where(past, gate, NEG_INF)
    chosen = blk == own
    blk_f = blk.astype(F32)
    for _ in range(MOBA_TOPK):
        best = jnp.max(gate, axis=0, keepdims=True)
        first = jnp.min(jnp.where(gate == best, blk_f, float(nb)), axis=0, keepdims=True)
        hit = blk_f == first
        chosen = chosen | (hit & past)
        gate = jnp.where(hit, -jnp.inf, gate)
    selb_t = jnp.where(chosen, 0.0, NEG_INF)
    selb_t = jnp.concatenate([selb_t, jnp.zeros((LANE - nb, B), F32)], axis=0)
    q_t = (qn * (ATTN_HEAD_DIM ** -0.5 * LOG2E)).T
    q_aug_t = jnp.concatenate([q_t.astype(BF16), selb_t.astype(BF16)], axis=0)

    def attend(n_blocks):
        m = jnp.full((1, B), NEG_INF, F32)
        for j0 in range(0, n_blocks, ATTN_QK_BLOCKS):
            s_all = _dot(kaug_ref[j0 * B:(j0 + ATTN_QK_BLOCKS) * B, :], q_aug_t)
            for u in range(ATTN_QK_BLOCKS):
                j = j0 + u
                tile = jnp.where(j > own, N_BIAS_TILES - 1, jnp.minimum(own - j, N_BIAS_TILES - 2))
                s = s_all[u * B:(u + 1) * B] + bias_ref[tile]
                s_ref[j] = s
                m = jnp.maximum(m, jnp.max(s, axis=0, keepdims=True))
        acc = jnp.zeros((ATTN_VT_ROWS, B), F32)
        for j in range(n_blocks):
            p = jnp.exp2(s_ref[j] - m)
            acc = acc + _dot(vaug_ref[:, j * B:(j + 1) * B], p.astype(BF16))
        out_t = acc[:ATTN_HEAD_DIM] / acc[ATTN_HEAD_DIM:ATTN_HEAD_DIM + 1]
        o_ref[...] = out_t.T.astype(o_ref.dtype)

    n_branches = -(-nb // ATTN_UNROLL)
    for i in range(n_branches):
        @pl.when(own // ATTN_UNROLL == i)
        def _(i=i):
            attend(min((i + 1) * ATTN_UNROLL, nb))


def _attn(plain, qg, kg, bias_tiles):
    t = plain.shape[0]
    B = MOBA_BLOCK
    hd = ATTN_HEAD_DIM
    base = (SSD_D_INNER + GDN_HEADS * GDN_DV) // hd
    return pl.pallas_call(
        _attn_kernel,
        grid=(ATTN_HEADS, t // B),
        in_specs=[pl.BlockSpec((B, hd), lambda h, i: (i, base + h)),
                  pl.BlockSpec((t, hd), lambda h, i: (0, base + ATTN_HEADS + h)),
                  pl.BlockSpec((t, hd), lambda h, i: (0, base + 2 * ATTN_HEADS + h)),
                  pl.BlockSpec((1, hd), lambda h, i: (0, 0)),
                  pl.BlockSpec((1, hd), lambda h, i: (0, 0)),
                  pl.BlockSpec((None, N_BIAS_TILES, B, B), lambda h, i: (h, 0, 0, 0))],
        out_specs=pl.BlockSpec((B, hd), lambda h, i: (i, h)),
        out_shape=jax.ShapeDtypeStruct((t, ATTN_HEADS * hd), BF16),
        scratch_shapes=[pltpu.VMEM((t, 2 * LANE), BF16), pltpu.VMEM((ATTN_VT_ROWS, t), BF16),
                        pltpu.VMEM((t // B, hd), F32), pltpu.VMEM((t // B, B, B), F32)],
        compiler_params=_params(("arbitrary", "arbitrary"), 56),
        name="moba",
    )(plain, plain, plain, qg, kg, bias_tiles)


def _bias_tiles(rel_bias, t):
    B = MOBA_BLOCK
    dist = jnp.arange(t, dtype=jnp.int32)
    exact = REL_BUCKETS // 2
    nf = jnp.maximum(dist, 1).astype(F32)
    large = exact + (jnp.log(nf / exact) / math.log(REL_MAX_DIST / exact) * (REL_BUCKETS - exact)).astype(jnp.int32)
    bucket = jnp.where(dist < exact, dist, jnp.minimum(large, REL_BUCKETS - 1))
    by_dist = rel_bias.T[:, bucket] * LOG2E
    nh = by_dist.shape[0]
    i = jnp.arange(B)[:, None]
    j = jnp.arange(B)[None, :]
    tiles = []
    for d in range(N_BIAS_TILES - 2):
        idx = jnp.clip(d * B + jnp.arange(2 * B - 1) - (B - 1), 0, t - 1)
        row = jnp.pad(by_dist[:, idx], ((0, 0), (0, 1)))
        skew = jnp.broadcast_to(row[:, None, :], (nh, B, 2 * B)).reshape(nh, 2 * B * B)
        tile = skew[:, :B * (2 * B - 1)].reshape(nh, B, 2 * B - 1)[:, :, B - 1:]
        if d == 0:
            tile = jnp.where(j >= i, tile, NEG_INF)
        tiles.append(tile)
    far = jnp.broadcast_to((rel_bias.T[:, REL_BUCKETS - 1] * LOG2E)[:, None, None], tiles[0].shape)
    tiles.append(far)
    tiles.append(jnp.full_like(far, NEG_INF))
    return jnp.stack(tiles, axis=1)


def _merge_kernel(y1_ref, y2_ref, y3_ref, w1_ref, w2_ref, w3_ref, g1_ref, g2_ref, g3_ref, o_ref):
    acc = _sigmoid(g1_ref[...]) * _dot(y1_ref[...], w1_ref[...])
    acc = acc + _sigmoid(g2_ref[...]) * _dot(y2_ref[...], w2_ref[...])
    acc = acc + _sigmoid(g3_ref[...]) * _dot(y3_ref[...], w3_ref[...])
    o_ref[...] = acc.astype(o_ref.dtype)


def _merge(y_ssd, y_gdn, y_attn, w1, w2, w3, plain, l):
    t, d = y_ssd.shape
    tm, tn = 512, 512
    gbase = (SSD_D_INNER + GDN_HEADS * GDN_DV + ATTN_QKV) // tn
    yspec = pl.BlockSpec((tm, d), lambda i, j: (i, 0))
    wspec = pl.BlockSpec((None, d, tn), lambda i, j: (l, 0, j))
    gspec = lambda b: pl.BlockSpec((tm, tn), lambda i, j: (i, gbase + b * (D_MODEL // tn) + j))
    return pl.pallas_call(
        _merge_kernel,
        grid=(t // tm, D_MODEL // tn),
        in_specs=[yspec, yspec, yspec, wspec, wspec, wspec, gspec(0), gspec(1), gspec(2)],
        out_specs=pl.BlockSpec((tm, tn), lambda i, j: (i, j)),
        out_shape=jax.ShapeDtypeStruct((t, D_MODEL), BF16),
        compiler_params=_params(("parallel", "parallel"), 48),
        name="merge",
    )(y_ssd, y_gdn, y_attn, w1, w2, w3, plain, plain, plain)


def _out_kernel(x_ref, w_ref, h_ref, g_ref, o_ref):
    o_ref[...] = h_ref[...] + g_ref[...] * _dot(x_ref[...], w_ref[...])


def _out_proj(x, w, h, g, l):
    t, k = x.shape
    n = w.shape[-1]
    tm, tn = 1024, 1024
    return pl.pallas_call(
        _out_kernel,
        grid=(n // tn, t // tm),
        in_specs=[pl.BlockSpec((tm, k), lambda j, i: (i, 0)),
                  pl.BlockSpec((None, k, tn), lambda j, i: (l, 0, j)),
                  pl.BlockSpec((tm, tn), lambda j, i: (i, j)),
                  pl.BlockSpec((1, tn), lambda j, i: (0, j))],
        out_specs=pl.BlockSpec((tm, tn), lambda j, i: (i, j)),
        out_shape=jax.ShapeDtypeStruct((t, n), F32),
        compiler_params=_params(("parallel", "parallel"), 48),
        name="out_proj",
    )(x, w, h, g)


def _pad_lanes(w, n=LANE):
    return jnp.pad(w, [(0, 0)] * (w.ndim - 1) + [(0, n - w.shape[-1])])


def _split_in(w_in):
    cuts = []
    lo = 0
    for s in IN_SPLITS:
        cuts.append(w_in[..., lo:lo + s])
        lo += s
    return cuts


def kernel(x, c, w_mod, b_mod, norm_ffn1, ffn1_w_gate, ffn1_w_up, ffn1_w_down, norm_mix, w_in, ssd_conv_w, ssd_conv_b, ssd_dt_bias, ssd_a_log, ssd_d, ssd_norm, w_o_ssd, gdn_conv_w, gdn_dt_bias, gdn_a_log, gdn_norm, w_o_gdn, attn_q_norm, attn_k_norm, rel_bias, w_o_attn, w_out, norm_ffn2, ffn2_w_gate, ffn2_w_up, ffn2_w_down):
    bsz, t, d = x.shape
    assert bsz == 1 and d == D_MODEL and t % 1024 == 0 and t // MOBA_BLOCK <= LANE
    depth = w_mod.shape[0]

    ssd_z, ssd_xbc, ssd_dt, gdn_qkv, gdn_z, gdn_a, gdn_b, attn_qkv, gates = _split_in(w_in)
    w_conv = jnp.concatenate([gdn_qkv, ssd_xbc], axis=-1).astype(BF16)
    w_plain = jnp.concatenate([ssd_z, gdn_z, attn_qkv, gates], axis=-1).astype(BF16)
    w_small = jnp.concatenate([_pad_lanes(ssd_dt), _pad_lanes(gdn_a[..., 0::2]), _pad_lanes(gdn_a[..., 1::2]),
                               _pad_lanes(gdn_b[..., 0::2]), _pad_lanes(gdn_b[..., 1::2])], axis=-1).astype(BF16)
    conv_w = jnp.concatenate([gdn_conv_w, ssd_conv_w], axis=-1)
    conv_b = jnp.concatenate([jnp.zeros((depth, GDN_QKV), F32), ssd_conv_b], axis=-1).reshape(depth, 1, CONV_N)
    ssd_dtb = _pad_lanes(ssd_dt_bias)
    ssd_alog = _pad_lanes(ssd_a_log)
    ssd_dskip = jnp.repeat(ssd_d, SSD_HEAD_DIM, axis=-1)
    gdn_dtb = jnp.stack([_pad_lanes(gdn_dt_bias[:, 0::2]), _pad_lanes(gdn_dt_bias[:, 1::2])], axis=1)
    gdn_alog = jnp.stack([_pad_lanes(gdn_a_log[:, 0::2]), _pad_lanes(gdn_a_log[:, 1::2])], axis=1)
    f1g, f1u, f1d = ffn1_w_gate.astype(BF16), ffn1_w_up.astype(BF16), ffn1_w_down.astype(BF16)
    f2g, f2u, f2d = ffn2_w_gate.astype(BF16), ffn2_w_up.astype(BF16), ffn2_w_down.astype(BF16)
    wo_ssd, wo_gdn, wo_attn, wo = (w_o_ssd.astype(BF16), w_o_gdn.astype(BF16), w_o_attn.astype(BF16),
                                   w_out.astype(BF16))
    bias_tiles = _bias_tiles(rel_bias, t)

    mod = _mod_all(c, w_mod, b_mod).reshape(depth, N_MOD, d)
    h = x.reshape(t, d)
    for l in range(depth):
        sh1, sc1, g1, sh2, sc2, g2, sh3, sc3, g3 = [mod[l, i][None, :] for i in range(N_MOD)]
        h = _ffn(h, norm_ffn1[l][None, :], sh1, sc1, g1, f1g, f1u, f1d, l)
        u = _modulate(h, norm_mix[l][None, :], sh2, sc2)
        conv = _matmul_conv(u, w_conv, conv_w, conv_b, l)
        plain = _matmul(u, w_plain, l, tn=1024)
        small = _matmul(u, w_small, l, tn=SMALL_N)
        y_ssd = _ssd(conv, plain, small, ssd_dtb[l][None, :], ssd_alog[l][None, :], ssd_dskip[l][None, :],
                     ssd_norm[l][None, :])
        y_gdn = _gdn(conv, plain, small, gdn_dtb[l], gdn_alog[l], gdn_norm[l][None, :])
        y_attn = _attn(plain, attn_q_norm[l][None, :], attn_k_norm[l][None, :], bias_tiles)
        merged = _merge(y_ssd, y_gdn, y_attn, wo_ssd, wo_gdn, wo_attn, plain, l)
        h = _out_proj(merged, wo, h, g2, l)
        h = _ffn(h, norm_ffn2[l][None, :], sh3, sc3, g3, f2g, f2u, f2d, l)
    return h.reshape(bsz, t, d)
```

```python
import functools
import math

import jax
import jax.numpy as jnp
from jax import lax
from jax.experimental import pallas as pl
from jax.experimental.pallas import tpu as pltpu

F32 = jnp.float32
BF16 = jnp.bfloat16

D_MODEL = 2048
DEPTH = 4
D_FF = 5632
CONV_K = 4
SSD_HEADS = 32
SSD_HEAD_DIM = 64
SSD_D_INNER = SSD_HEADS * SSD_HEAD_DIM
SSD_GROUPS = 4
SSD_STATE = 128
SSD_CHUNK = 128
GDN_HEADS = 16
GDN_DK = 128
GDN_DV = 128
GDN_CHUNK = 64
ATTN_HEADS = 16
ATTN_HEAD_DIM = 128
MOBA_BLOCK = 256
MOBA_TOPK = 3
REL_BUCKETS = 32
REL_MAX_DIST = 4096
N_BRANCHES = 3
N_MOD = 9
EPS = 1e-6
NEG_INF = -1e30

SSD_XBC = SSD_D_INNER + 2 * SSD_GROUPS * SSD_STATE
GDN_QKV = GDN_HEADS * (2 * GDN_DK + GDN_DV)
ATTN_QKV = 3 * ATTN_HEADS * ATTN_HEAD_DIM
IN_SPLITS = (SSD_D_INNER, SSD_XBC, SSD_HEADS, GDN_QKV, GDN_HEADS * GDN_DV, GDN_HEADS, GDN_HEADS, ATTN_QKV,
             N_BRANCHES * D_MODEL)

LANE = 128
V7X_VMEM_BYTES = 64 * 1024 * 1024
N_BIAS_TILES = 15
ATTN_UNROLL = 8
ATTN_QK_BLOCKS = 4
GDN_CHUNKS_PER_STEP = 2
ATTN_VT_ROWS = ATTN_HEAD_DIM + 16
LOG2E = 1.4426950408889634
CONV_COL_CHUNK = 256
CONV_N = GDN_QKV + SSD_XBC
PLAIN_N = SSD_D_INNER + GDN_HEADS * GDN_DV + ATTN_QKV + N_BRANCHES * D_MODEL
SMALL_N = 5 * LANE


def _params(semantics, vmem_mb):
    return pltpu.CompilerParams(dimension_semantics=semantics, vmem_limit_bytes=vmem_mb * 1024 * 1024)


def _sigmoid(x):
    return 1.0 / (1.0 + jnp.exp(-x))


def _softplus(x):
    return jnp.maximum(x, 0.0) + jnp.log1p(jnp.exp(-jnp.abs(x)))


def _dot(a, b):
    return jnp.dot(a, b, preferred_element_type=F32)


def _dot_nt(a, b):
    return lax.dot_general(a, b, (((1,), (1,)), ((), ())), preferred_element_type=F32)


def _dot_tn(a, b):
    return lax.dot_general(a, b, (((0,), (0,)), ((), ())), preferred_element_type=F32)


def _dot_hi(a, b):
    return jnp.dot(a, b, preferred_element_type=F32, precision=lax.Precision.HIGHEST)


def _mod_kernel(cb_ref, w_ref, b_ref, o_ref):
    cb = cb_ref[...]
    tn = w_ref.shape[-1]
    for j in range(tn // LANE):
        sl = slice(j * LANE, (j + 1) * LANE)
        o_ref[:, sl] = jnp.sum(w_ref[:, sl] * cb, axis=0, keepdims=True) + b_ref[:, sl]


def _mod_all(c, w_mod, b_mod):
    depth, d, n = w_mod.shape
    tn = 1024
    cb = jnp.broadcast_to(c.reshape(d, 1), (d, LANE))
    return pl.pallas_call(
        _mod_kernel,
        grid=(depth, n // tn),
        in_specs=[pl.BlockSpec((d, LANE), lambda l, j: (0, 0)),
                  pl.BlockSpec((None, d, tn), lambda l, j: (l, 0, j)),
                  pl.BlockSpec((None, 1, tn), lambda l, j: (l, 0, j))],
        out_specs=pl.BlockSpec((None, 1, tn), lambda l, j: (l, 0, j)),
        out_shape=jax.ShapeDtypeStruct((depth, 1, n), F32),
        compiler_params=_params(("parallel", "parallel"), 40),
        name="adaln_mod",
    )(cb, w_mod, b_mod.reshape(depth, 1, n))


def _modulated(x, gn, sh, sc):
    xn = x * lax.rsqrt(jnp.mean(x * x, axis=-1, keepdims=True) + EPS)
    return (xn * gn) * (1.0 + sc) + sh


def _modulate_kernel(h_ref, gn_ref, sh_ref, sc_ref, o_ref):
    o_ref[...] = _modulated(h_ref[...], gn_ref[...], sh_ref[...], sc_ref[...]).astype(o_ref.dtype)


def _modulate(h, gn, sh, sc):
    t, d = h.shape
    tm = 512
    vec = pl.BlockSpec((1, d), lambda i: (0, 0))
    return pl.pallas_call(
        _modulate_kernel,
        grid=(t // tm,),
        in_specs=[pl.BlockSpec((tm, d), lambda i: (i, 0)), vec, vec, vec],
        out_specs=pl.BlockSpec((tm, d), lambda i: (i, 0)),
        out_shape=jax.ShapeDtypeStruct((t, d), BF16),
        compiler_params=_params(("parallel",), 32),
        name="modulate",
    )(h, gn, sh, sc)


def _ffn_kernel(h_ref, gn_ref, sh_ref, sc_ref, g_ref, wg_ref, wu_ref, wd_ref, o_ref, xm_ref):
    f = pl.program_id(1)

    @pl.when(f == 0)
    def _():
        xm_ref[...] = _modulated(h_ref[...], gn_ref[...], sh_ref[...], sc_ref[...]).astype(BF16)
        o_ref[...] = jnp.zeros_like(o_ref)

    xm = xm_ref[...]
    a = _dot(xm, wg_ref[...])
    b = _dot(xm, wu_ref[...])
    hm = ((a * _sigmoid(a)) * b).astype(BF16)
    o_ref[...] += _dot(hm, wd_ref[...])

    @pl.when(f == pl.num_programs(1) - 1)
    def _():
        o_ref[...] = h_ref[...] + (0.5 * g_ref[...]) * o_ref[...]


def _ffn(h, gn, sh, sc, g, wg, wu, wd, l):
    t, d = h.shape
    ff = wg.shape[-1]
    tm, tf = 512, 512
    vec = pl.BlockSpec((1, d), lambda i, f: (0, 0))
    return pl.pallas_call(
        _ffn_kernel,
        grid=(t // tm, ff // tf),
        in_specs=[pl.BlockSpec((tm, d), lambda i, f: (i, 0)), vec, vec, vec, vec,
                  pl.BlockSpec((None, d, tf), lambda i, f: (l, 0, f)),
                  pl.BlockSpec((None, d, tf), lambda i, f: (l, 0, f)),
                  pl.BlockSpec((None, tf, d), lambda i, f: (l, f, 0))],
        out_specs=pl.BlockSpec((tm, d), lambda i, f: (i, 0)),
        out_shape=jax.ShapeDtypeStruct((t, d), F32),
        scratch_shapes=[pltpu.VMEM((tm, d), BF16)],
        compiler_params=_params(("parallel", "arbitrary"), 48),
        name="ffn",
    )(h, gn, sh, sc, g, wg, wu, wd)


def _mm_kernel(x_ref, w_ref, o_ref):
    o_ref[...] = _dot(x_ref[...], w_ref[...]).astype(o_ref.dtype)


def _matmul(x, w, l, tn, tm=1024, out_dtype=F32):
    t, k = x.shape
    n = w.shape[-1]
    return pl.pallas_call(
        _mm_kernel,
        grid=(n // tn, t // tm),
        in_specs=[pl.BlockSpec((tm, k), lambda j, i: (i, 0)),
                  pl.BlockSpec((None, k, tn), lambda j, i: (l, 0, j))],
        out_specs=pl.BlockSpec((tm, tn), lambda j, i: (i, j)),
        out_shape=jax.ShapeDtypeStruct((t, n), out_dtype),
        compiler_params=_params(("parallel", "parallel"), 48),
        name="proj",
    )(x, w)


def _segment_row(j, tn, segments):
    col = j * tn
    row = col
    out_lo = 0
    for src, size in segments:
        assert size % tn == 0 and src % 8 == 0
        row = jnp.where(col >= out_lo, src + (col - out_lo), row)
        out_lo += size
    return pl.multiple_of(row, 8)


def _wt_spec(l, tn, k, segments):
    return pl.BlockSpec((pl.Element(1), pl.Element(tn), pl.Element(k)),
                        lambda j, i: (l, _segment_row(j, tn, segments), 0))


def _mm_t_kernel(x_ref, wt_ref, o_ref, wb_ref):
    @pl.when(pl.program_id(1) == 0)
    def _():
        wb_ref[...] = wt_ref[0].astype(BF16)

    o_ref[...] = _dot_nt(x_ref[...], wb_ref[...]).astype(o_ref.dtype)


def _matmul_t(x, w_t, l, segments, tn=1024, tm=1024):
    t, k = x.shape
    n = sum(size for _, size in segments)
    return pl.pallas_call(
        _mm_t_kernel,
        grid=(n // tn, t // tm),
        in_specs=[pl.BlockSpec((tm, k), lambda j, i: (i, 0)), _wt_spec(l, tn, k, segments)],
        out_specs=pl.BlockSpec((tm, tn), lambda j, i: (i, j)),
        out_shape=jax.ShapeDtypeStruct((t, n), F32),
        scratch_shapes=[pltpu.VMEM((tn, k), BF16)],
        compiler_params=_params(("parallel", "arbitrary"), 48),
        name="proj",
    )(x, w_t)


def _mm_conv_kernel(x_ref, wt_ref, cw_ref, cb_ref, o_ref, win_ref, wb_ref):
    i = pl.program_id(1)

    @pl.when(i == 0)
    def _():
        win_ref[:, 0:8, :] = jnp.zeros((win_ref.shape[0], 8, win_ref.shape[2]), F32)
        wb_ref[...] = wt_ref[0].astype(BF16)

    tm, tn = o_ref.shape
    cc = CONV_COL_CHUNK
    n_chunks = tn // cc

    def project(c):
        win_ref[c, 8:8 + tm, :] = _dot_nt(x_ref[...], wb_ref[c * cc:(c + 1) * cc, :])

    def conv(c):
        cs = slice(c * cc, (c + 1) * cc)
        cw = cw_ref[:, cs]
        y = cb_ref[:, cs] + cw[CONV_K - 1:CONV_K, :] * win_ref[c, 8:8 + tm, :]
        for s in range(1, CONV_K):
            y = y + cw[CONV_K - 1 - s:CONV_K - s, :] * win_ref[c, 8 - s:8 - s + tm, :]
        o_ref[:, cs] = y * _sigmoid(y)
        win_ref[c, 0:8, :] = win_ref[c, tm:tm + 8, :]

    project(0)
    for c in range(1, n_chunks):
        project(c)
        conv(c - 1)
    conv(n_chunks - 1)


def _matmul_conv(x, w_t, cw, cb, l, segments, tn=1024, tm=1024):
    t, k = x.shape
    n = sum(size for _, size in segments)
    return pl.pallas_call(
        _mm_conv_kernel,
        grid=(n // tn, t // tm),
        in_specs=[pl.BlockSpec((tm, k), lambda j, i: (i, 0)),
                  _wt_spec(l, tn, k, segments),
                  pl.BlockSpec((None, CONV_K, tn), lambda j, i: (l, 0, j)),
                  pl.BlockSpec((None, 1, tn), lambda j, i: (l, 0, j))],
        out_specs=pl.BlockSpec((tm, tn), lambda j, i: (i, j)),
        out_shape=jax.ShapeDtypeStruct((t, n), F32),
        scratch_shapes=[pltpu.VMEM((tn // CONV_COL_CHUNK, tm + 8, CONV_COL_CHUNK), F32),
                        pltpu.VMEM((tn, k), BF16)],
        compiler_params=_params(("parallel", "arbitrary"), 56),
        name="proj_conv",
    )(x, w_t, cw, cb)


def _ssd_kernel(xbc_ref, z_ref, dt_ref, dtb_ref, alog_ref, dskip_ref, ng_ref, o_ref, state_ref):
    c = pl.program_id(0)

    @pl.when(c == 0)
    def _():
        state_ref[...] = jnp.zeros_like(state_ref)

    L = SSD_CHUNK
    P = SSD_HEAD_DIM
    gw = SSD_D_INNER // SSD_GROUPS
    hpg = SSD_HEADS // SSD_GROUPS
    dt = _softplus(dt_ref[...] + dtb_ref[...])
    a = -jnp.exp(alog_ref[...])
    da = dt * a
    r_i = lax.broadcasted_iota(jnp.int32, (L, L), 0)
    c_i = lax.broadcasted_iota(jnp.int32, (L, L), 1)
    tri = r_i >= c_i
    acs = _dot_hi(tri.astype(F32), da)
    acs_t = acs.T
    last = acs[L - 1:L, :]
    e_acs = jnp.exp(acs)
    e_dec = jnp.exp(last - acs)
    e_last = jnp.exp(last)
    lane = lax.broadcasted_iota(jnp.int32, (L, LANE), 1)
    lo_half = lane < P

    def pair_bcast(v, h):
        rows = v.shape[0]
        return jnp.where(lo_half[:rows], jnp.broadcast_to(v[:, h:h + 1], (rows, LANE)),
                         jnp.broadcast_to(v[:, h + 1:h + 2], (rows, LANE)))

    for g in range(SSD_GROUPS):
        bg = xbc_ref[:, SSD_D_INNER + g * SSD_STATE:SSD_D_INNER + (g + 1) * SSD_STATE].astype(BF16)
        cg = xbc_ref[:, SSD_D_INNER + SSD_GROUPS * SSD_STATE + g * SSD_STATE:
                     SSD_D_INNER + SSD_GROUPS * SSD_STATE + (g + 1) * SSD_STATE].astype(BF16)
        cb = _dot_nt(cg, bg)
        s_in = state_ref[g]
        y_off = _dot(cg, s_in.astype(BF16))
        xdec_parts = []
        dec_parts = []
        y_parts = []
        for hp in range(hpg // 2):
            h = g * hpg + 2 * hp
            ch = slice(h * P, (h + 2) * P)
            xs = xbc_ref[:, ch]
            xdt = xs * pair_bcast(dt, h)
            gmats = []
            for hh in (h, h + 1):
                diff = acs[:, hh:hh + 1] - acs_t[hh:hh + 1, :]
                lm = jnp.exp(jnp.where(tri, diff, -jnp.inf))
                gmats.append((lm * cb).astype(BF16))
            lhs = jnp.concatenate(gmats, axis=1)
            rhs = jnp.concatenate([jnp.where(lo_half, xdt, 0.0), jnp.where(lo_half, 0.0, xdt)], axis=0).astype(BF16)
            y = _dot(lhs, rhs)
            y = y + y_off[:, 2 * hp * P:(2 * hp + 2) * P] * pair_bcast(e_acs, h)
            y = y + xs * dskip_ref[:, ch]
            zz = z_ref[:, ch]
            y_parts.append(y * (zz * _sigmoid(zz)))
            xdec_parts.append((xdt * pair_bcast(e_dec, h)).astype(BF16))
            dec_parts.append(pair_bcast(e_last, h))
        xdec = jnp.concatenate(xdec_parts, axis=1)
        dec = jnp.concatenate(dec_parts, axis=1)
        state_ref[g] = s_in * dec + _dot_tn(bg, xdec)
        ssq = sum(jnp.sum(y * y, axis=-1, keepdims=True) for y in y_parts)
        inv_rms = lax.rsqrt(ssq * (1.0 / gw) + EPS)
        for i, y in enumerate(y_parts):
            ch = slice(g * gw + i * LANE, g * gw + (i + 1) * LANE)
            o_ref[:, ch] = (y * inv_rms * ng_ref[:, ch]).astype(o_ref.dtype)


def _ssd(conv, plain, small, dtb, alog, dskip, ng):
    t = conv.shape[0]
    L = SSD_CHUNK
    xbc_blk = GDN_QKV // SSD_XBC
    vec = lambda n: pl.BlockSpec((1, n), lambda c: (0, 0))
    return pl.pallas_call(
        _ssd_kernel,
        grid=(t // L,),
        in_specs=[pl.BlockSpec((L, SSD_XBC), lambda c: (c, xbc_blk)),
                  pl.BlockSpec((L, SSD_D_INNER), lambda c: (c, 0)),
                  pl.BlockSpec((L, LANE), lambda c: (c, 0)),
                  vec(LANE), vec(LANE), vec(SSD_D_INNER), vec(SSD_D_INNER)],
        out_specs=pl.BlockSpec((L, SSD_D_INNER), lambda c: (c, 0)),
        out_shape=jax.ShapeDtypeStruct((t, SSD_D_INNER), BF16),
        scratch_shapes=[pltpu.VMEM((SSD_GROUPS, SSD_STATE, SSD_D_INNER // SSD_GROUPS), F32)],
        compiler_params=_params(("arbitrary",), 32),
        name="ssd",
    )(conv, plain, small, dtb, alog, dskip, ng)


def _gdn_kernel(qkv_ref, z_ref, ae_ref, ao_ref, be_ref, bo_ref, dtb_ref, alog_ref, ng_ref, o_ref, state_ref):
    c = pl.program_id(0)

    @pl.when(c == 0)
    def _():
        state_ref[...] = jnp.zeros_like(state_ref)

    L = GDN_CHUNK
    R = 2 * L
    n_pairs = GDN_HEADS // 2
    n_ch = qkv_ref.shape[0] // L
    row1 = lax.broadcasted_iota(jnp.int32, (R, LANE), 0)
    top = row1 < L
    dtb2 = jnp.where(top, dtb_ref[0:1, :], dtb_ref[1:2, :])
    alog2 = jnp.where(top, alog_ref[0:1, :], alog_ref[1:2, :])
    r_i = lax.broadcasted_iota(jnp.int32, (R, R), 0)
    c_i = lax.broadcasted_iota(jnp.int32, (R, R), 1)
    same = (r_i < L) == (c_i < L)
    incl = same & (r_i >= c_i)
    strict = same & (r_i > c_i)
    incl_f = incl.astype(F32)
    eye = (r_i == c_i).astype(F32)
    merge_masks = []
    for lb in range(int(math.log2(L))):
        r_blk = jnp.right_shift(r_i, lb)
        c_blk = jnp.right_shift(c_i, lb)
        merge_masks.append((r_blk == c_blk + 1) & (jnp.bitwise_and(r_blk, 1) == 1))
    row2 =lax.broadcasted_iota(jnp.int32, (R, 2 * LANE), 0)
    col2 = lax.broadcasted_iota(jnp.int32, (R, 2 * LANE), 1)
    diag2 = (row2 < L) == (col2 < LANE)
    srow = lax.broadcasted_iota(jnp.int32, (2 * GDN_DK, LANE), 0)

    def stack(ref, base, p, ch):
        lo = base + 2 * p * LANE
        rows = slice(ch * L, (ch + 1) * L)
        return jnp.concatenate([ref[rows, lo:lo + LANE], ref[rows, lo + LANE:lo + 2 * LANE]], axis=0)

    def blockdiag(x):
        return jnp.where(diag2, jnp.concatenate([x, x], axis=1), 0.0).astype(BF16)

    def col(v, p):
        return jnp.broadcast_to(v[:, p:p + 1], (R, LANE))

    pairs = range(n_pairs)
    units = [(ch, p) for ch in range(n_ch) for p in pairs]
    a_b, qk_b, rhs_b, q2, k2, e_gl = {}, {}, {}, {}, {}, {}
    for ch in range(n_ch):
        rows = slice(ch * L, (ch + 1) * L)
        a2 = jnp.concatenate([ae_ref[rows, :], ao_ref[rows, :]], axis=0)
        b2 = jnp.concatenate([be_ref[rows, :], bo_ref[rows, :]], axis=0)
        beta2 = _sigmoid(b2)
        g2 = -jnp.exp(alog2) * _softplus(a2 + dtb2)
        gc2 = _dot_hi(incl_f, g2)
        gc2_t = gc2.T
        gl2 = jnp.where(top, gc2[L - 1:L, :], gc2[R - 1:R, :])
        e_gc = jnp.exp(gc2)
        e_rem = jnp.exp(gl2 - gc2)
        e_gl[ch] = jnp.exp(gl2)
        for p in pairs:
            u = (ch, p)
            q = stack(qkv_ref, 0, p, ch)
            k = stack(qkv_ref, GDN_HEADS * GDN_DK, p, ch)
            v = stack(qkv_ref, 2 * GDN_HEADS * GDN_DK, p, ch)
            qn = q * lax.rsqrt(jnp.sum(q * q, axis=-1, keepdims=True) + EPS) * (GDN_DK ** -0.5)
            kn = k * lax.rsqrt(jnp.sum(k * k, axis=-1, keepdims=True) + EPS)
            beta = col(beta2, p)
            diff = gc2[:, p:p + 1] - gc2_t[p:p + 1, :]
            decay = jnp.exp(jnp.where(incl, diff, -jnp.inf))
            kb = kn * beta
            kn_b = kn.astype(BF16)
            a_b[u] = jnp.where(strict, _dot_nt(kb.astype(BF16), kn_b) * decay, 0.0).astype(BF16)
            qk_b[u] = jnp.where(incl, _dot_nt(qn.astype(BF16), kn_b) * decay, 0.0).astype(BF16)
            rhs_b[u] = jnp.concatenate([v * beta, kb * col(e_gc, p)], axis=1).astype(BF16)
            q2[u] = blockdiag(qn * col(e_gc, p))
            k2[u] = blockdiag(kn * col(e_rem, p))
    zero_b = jnp.zeros((R, R), BF16)
    inv = {u: eye - jnp.where(merge_masks[0], a_b[u], zero_b).astype(F32) for u in units}
    for mk in merge_masks[1:]:
        inv_b = {u: inv[u].astype(BF16) for u in units}
        de = {u: _dot(inv_b[u], jnp.where(mk, a_b[u], zero_b)).astype(BF16) for u in units}
        inv = {u: inv[u] - _dot(de[u], inv_b[u]) for u in units}
    sol = {u: _dot(inv[u].astype(BF16), rhs_b[u]) for u in units}
    s = [state_ref[p] for p in pairs]
    o = {}
    for ch in range(n_ch):
        s_b = [s[p].astype(BF16) for p in pairs]
        v_cb = [(sol[ch, p][:, :GDN_DV] - _dot(blockdiag(sol[ch, p][:, GDN_DV:]), s_b[p])).astype(BF16)
                for p in pairs]
        for p in pairs:
            o[ch, p] = _dot(q2[ch, p], s_b[p]) + _dot(qk_b[ch, p], v_cb[p])
        for p in pairs:
            sdec = jnp.where(srow < GDN_DK, jnp.broadcast_to(e_gl[ch][0:1, p:p + 1], (2 * GDN_DK, LANE)),
                             jnp.broadcast_to(e_gl[ch][R - 1:R, p:p + 1], (2 * GDN_DK, LANE)))
            s[p] = s[p] * sdec + _dot_tn(k2[ch, p], v_cb[p])
    for p in pairs:
        state_ref[p] = s[p]
    for ch, p in units:
        z = stack(z_ref, 0, p, ch)
        on = o[ch, p] * lax.rsqrt(jnp.mean(o[ch, p] * o[ch, p], axis=-1, keepdims=True) + EPS) * ng_ref[...]
        res = (on * (z * _sigmoid(z))).astype(o_ref.dtype)
        rows = slice(ch * L, (ch + 1) * L)
        o_ref[rows, 2 * p * LANE:(2 * p + 1) * LANE] = res[:L]
        o_ref[rows, (2 * p + 1) * LANE:(2 * p + 2) * LANE] = res[L:]


def _gdn(conv, plain, small, dtb, alog, ng):
    t = conv.shape[0]
    L = GDN_CHUNKS_PER_STEP * GDN_CHUNK
    dz = GDN_HEADS * GDN_DV
    sm = lambda j: pl.BlockSpec((L, LANE), lambda c: (c, j))
    return pl.pallas_call(
        _gdn_kernel,
        grid=(t // L,),
        in_specs=[pl.BlockSpec((L, GDN_QKV), lambda c: (c, 0)),
                  pl.BlockSpec((L, dz), lambda c: (c, SSD_D_INNER // dz)),
                  sm(1), sm(2), sm(3), sm(4),
                  pl.BlockSpec((2, LANE), lambda c: (0, 0)),
                  pl.BlockSpec((2, LANE), lambda c: (0, 0)),
                  pl.BlockSpec((1, GDN_DV), lambda c: (0, 0))],
        out_specs=pl.BlockSpec((L, dz), lambda c: (c, 0)),
        out_shape=jax.ShapeDtypeStruct((t, dz), BF16),
        scratch_shapes=[pltpu.VMEM((GDN_HEADS // 2, 2 * GDN_DK, GDN_DV), F32)],
        compiler_params=_params(("arbitrary",), 40),
        name="gdn",
    )(conv, plain, small, small, small, small, dtb, alog, ng)


def _attn_kernel(q_ref, k_ref, v_ref, qg_ref, kg_ref, bias_ref, o_ref, kaug_ref, vaug_ref, kmean_ref, s_ref):
    qt = pl.program_id(1)
    B = MOBA_BLOCK
    t = k_ref.shape[0]
    nb = t // B

    @pl.when(qt == 0)
    def _():
        lane_b = lax.broadcasted_iota(jnp.int32, (B, LANE), 1)
        vaug_ref[ATTN_HEAD_DIM:ATTN_VT_ROWS, :] = jnp.ones((ATTN_VT_ROWS - ATTN_HEAD_DIM, t), BF16)
        for b in range(nb):
            rows = slice(b * B, (b + 1) * B)
            k = k_ref[rows, :]
            kn = k * lax.rsqrt(jnp.mean(k * k, axis=-1, keepdims=True) + EPS) * kg_ref[...]
            kaug_ref[rows, 0:LANE] = kn.astype(BF16)
            kaug_ref[rows, LANE:2 * LANE] = jnp.where(lane_b == b, 1.0, 0.0).astype(BF16)
            vaug_ref[0:ATTN_HEAD_DIM, rows] = v_ref[rows, :].T.astype(BF16)
            kmean_ref[b:b + 1, :] = jnp.mean(kn, axis=0, keepdims=True)

    q = q_ref[...]
    qn = q * lax.rsqrt(jnp.mean(q * q, axis=-1, keepdims=True) + EPS) * qg_ref[...]
    own = qt
    gate = lax.dot_general(kmean_ref[...], qn, (((1,), (1,)), ((), ())), preferred_element_type=F32,
                           precision=lax.Precision.HIGHEST)
    blk = lax.broadcasted_iota(jnp.int32, gate.shape, 0)
    past = blk < own
    gate = jnp.where(past, gate, NEG_INF)
    chosen = blk == own
    blk_f = blk.astype(F32)
    for _ in range(MOBA_TOPK):
        best = jnp.max(gate, axis=0, keepdims=True)
        first = jnp.min(jnp.where(gate == best, blk_f, float(nb)), axis=0, keepdims=True)
        hit = blk_f == first
        chosen = chosen | (hit & past)
        gate = jnp.where(hit, -jnp.inf, gate)
    selb_t = jnp.where(chosen, 0.0, NEG_INF)
    selb_t = jnp.concatenate([selb_t, jnp.zeros((LANE - nb, B), F32)], axis=0)
    q_t = (qn * (ATTN_HEAD_DIM ** -0.5 * LOG2E)).T
    q_aug_t = jnp.concatenate([q_t.astype(BF16), selb_t.astype(BF16)], axis=0)

    def attend(n_blocks):
        m = jnp.full((1, B), NEG_INF, F32)
        for j0 in range(0, n_blocks, ATTN_QK_BLOCKS):
            s_all = _dot(kaug_ref[j0 * B:(j0 + ATTN_QK_BLOCKS) * B, :], q_aug_t)
            for u in range(ATTN_QK_BLOCKS):
                j = j0 + u
                tile = jnp.where(j > own, N_BIAS_TILES - 1, jnp.minimum(own - j, N_BIAS_TILES - 2))
                s = s_all[u * B:(u + 1) * B] + bias_ref[tile]
                s_ref[j] = s
                m = jnp.maximum(m, jnp.max(s, axis=0, keepdims=True))
        acc = jnp.zeros((ATTN_VT_ROWS, B), F32)
        for j in range(n_blocks):
            p = jnp.exp2(s_ref[j] - m)
            acc = acc + _dot(vaug_ref[:, j * B:(j + 1) * B], p.astype(BF16))
        out_t = acc[:ATTN_HEAD_DIM] / acc[ATTN_HEAD_DIM:ATTN_HEAD_DIM + 1]
        o_ref[...] = out_t.T.astype(o_ref.dtype)

    n_branches = -(-nb // ATTN_UNROLL)
    for i in range(n_branches):
        @pl.when(own // ATTN_UNROLL == i)
        def _(i=i):
            attend(min((i + 1) * ATTN_UNROLL, nb))


def _attn(plain, qg, kg, bias_tiles):
    t = plain.shape[0]
    B = MOBA_BLOCK
    hd = ATTN_HEAD_DIM
    base = (SSD_D_INNER + GDN_HEADS * GDN_DV) // hd
    return pl.pallas_call(
        _attn_kernel,
        grid=(ATTN_HEADS, t // B),
        in_specs=[pl.BlockSpec((B, hd), lambda h, i: (i, base + h)),
                  pl.BlockSpec((t, hd), lambda h, i: (0, base + ATTN_HEADS + h)),
                  pl.BlockSpec((t, hd), lambda h, i: (0, base + 2 * ATTN_HEADS + h)),
                  pl.BlockSpec((1, hd), lambda h, i: (0, 0)),
                  pl.BlockSpec((1, hd), lambda h, i: (0, 0)),
                  pl.BlockSpec((None, N_BIAS_TILES, B, B), lambda h, i: (h, 0, 0, 0))],
        out_specs=pl.BlockSpec((B, hd), lambda h, i: (i, h)),
        out_shape=jax.ShapeDtypeStruct((t, ATTN_HEADS * hd), BF16),
        scratch_shapes=[pltpu.VMEM((t, 2 * LANE), BF16), pltpu.VMEM((ATTN_VT_ROWS, t), BF16),
                        pltpu.VMEM((t // B, hd), F32), pltpu.VMEM((t // B, B, B), F32)],
        compiler_params=_params(("arbitrary", "arbitrary"), 56),
        name="moba",
    )(plain, plain, plain, qg, kg, bias_tiles)


def _bias_tiles(rel_bias, t):
    B = MOBA_BLOCK
    dist = jnp.arange(t, dtype=jnp.int32)
    exact = REL_BUCKETS // 2
    nf = jnp.maximum(dist, 1).astype(F32)
    large = exact + (jnp.log(nf / exact) / math.log(REL_MAX_DIST / exact) * (REL_BUCKETS - exact)).astype(jnp.int32)
    bucket = jnp.where(dist < exact, dist, jnp.minimum(large, REL_BUCKETS - 1))
    by_dist = rel_bias.T[:, bucket] * LOG2E
    nh = by_dist.shape[0]
    i = jnp.arange(B)[:, None]
    j = jnp.arange(B)[None, :]
    tiles = []
    for d in range(N_BIAS_TILES - 2):
        idx = jnp.clip(d * B + jnp.arange(2 * B - 1) - (B - 1), 0, t - 1)
        row = jnp.pad(by_dist[:, idx], ((0, 0), (0, 1)))
        skew = jnp.broadcast_to(row[:, None, :], (nh, B, 2 * B)).reshape(nh, 2 * B * B)
        tile = skew[:, :B * (2 * B - 1)].reshape(nh, B, 2 * B - 1)[:, :, B - 1:]
        if d == 0:
            tile = jnp.where(j >= i, tile, NEG_INF)
        tiles.append(tile)
    far = jnp.broadcast_to((rel_bias.T[:, REL_BUCKETS - 1] * LOG2E)[:, None, None], tiles[0].shape)
    tiles.append(far)
    tiles.append(jnp.full_like(far, NEG_INF))
    return jnp.stack(tiles, axis=1)


def _merge_kernel(y1_ref, y2_ref, y3_ref, w1_ref, w2_ref, w3_ref, g1_ref, g2_ref, g3_ref, o_ref):
    acc = _sigmoid(g1_ref[...]) * _dot(y1_ref[...], w1_ref[...])
    acc = acc + _sigmoid(g2_ref[...]) * _dot(y2_ref[...], w2_ref[...])
    acc = acc + _sigmoid(g3_ref[...]) * _dot(y3_ref[...], w3_ref[...])
    o_ref[...] = acc.astype(o_ref.dtype)


def _merge(y_ssd, y_gdn, y_attn, w1, w2, w3, plain, l):
    t, d = y_ssd.shape
    tm, tn = 512, 512
    gbase = (SSD_D_INNER + GDN_HEADS * GDN_DV + ATTN_QKV) // tn
    yspec = pl.BlockSpec((tm, d), lambda i, j: (i, 0))
    wspec = pl.BlockSpec((None, d, tn), lambda i, j: (l, 0, j))
    gspec = lambda b: pl.BlockSpec((tm, tn), lambda i, j: (i, gbase + b * (D_MODEL // tn) + j))
    return pl.pallas_call(
        _merge_kernel,
        grid=(t // tm, D_MODEL // tn),
        in_specs=[yspec, yspec, yspec, wspec, wspec, wspec, gspec(0), gspec(1), gspec(2)],
        out_specs=pl.BlockSpec((tm, tn), lambda i, j: (i, j)),
        out_shape=jax.ShapeDtypeStruct((t, D_MODEL), BF16),
        compiler_params=_params(("parallel", "parallel"), 48),
        name="merge",
    )(y_ssd, y_gdn, y_attn, w1, w2, w3, plain, plain, plain)


def _out_kernel(x_ref, w_ref, h_ref, g_ref, o_ref):
    o_ref[...] = h_ref[...] + g_ref[...] * _dot(x_ref[...], w_ref[...])


def _out_proj(x, w, h, g, l):
    t, k = x.shape
    n = w.shape[-1]
    tm, tn = 1024, 1024
    return pl.pallas_call(
        _out_kernel,
        grid=(n // tn, t // tm),
        in_specs=[pl.BlockSpec((tm, k), lambda j, i: (i, 0)),
                  pl.BlockSpec((None, k, tn), lambda j, i: (l, 0, j)),
                  pl.BlockSpec((tm, tn), lambda j, i: (i, j)),
                  pl.BlockSpec((1, tn), lambda j, i: (0, j))],
        out_specs=pl.BlockSpec((tm, tn), lambda j, i: (i, j)),
        out_shape=jax.ShapeDtypeStruct((t, n), F32),
        compiler_params=_params(("parallel", "parallel"), 48),
        name="out_proj",
    )(x, w, h, g)


def _pad_lanes(w, n=LANE):
    return jnp.pad(w, [(0, 0)] * (w.ndim - 1) + [(0, n - w.shape[-1])])


def _split_in(w_in):
    cuts = []
    lo = 0
    for s in IN_SPLITS:
        cuts.append(w_in[..., lo:lo + s])
        lo += s
    return cuts


def _in_segments():
    names = ("ssd_z", "ssd_xbc", "ssd_dt", "gdn_qkv", "gdn_z", "gdn_a", "gdn_b", "attn_qkv", "gates")
    out, lo = {}, 0
    for name, size in zip(names, IN_SPLITS):
        out[name] = (lo, size)
        lo += size
    return out


def kernel(x, c, w_mod, b_mod, norm_ffn1, ffn1_w_gate, ffn1_w_up, ffn1_w_down, norm_mix, w_in, ssd_conv_w, ssd_conv_b, ssd_dt_bias, ssd_a_log, ssd_d, ssd_norm, w_o_ssd, gdn_conv_w, gdn_dt_bias, gdn_a_log, gdn_norm, w_o_gdn, attn_q_norm, attn_k_norm, rel_bias, w_o_attn, w_out, norm_ffn2, ffn2_w_gate, ffn2_w_up, ffn2_w_down):
    bsz, t, d = x.shape
    assert bsz == 1 and d == D_MODEL and t % 1024 == 0 and t // MOBA_BLOCK <= LANE
    depth = w_mod.shape[0]

    w_in_t = jnp.swapaxes(w_in, 1, 2)
    seg = _in_segments()
    conv_segments = (seg["gdn_qkv"], seg["ssd_xbc"])
    plain_segments = (seg["ssd_z"], seg["gdn_z"], seg["attn_qkv"], seg["gates"])
    _, _, ssd_dt, _, _, gdn_a, gdn_b, _, _ = _split_in(w_in)
    w_small =jnp.concatenate([_pad_lanes(ssd_dt), _pad_lanes(gdn_a[..., 0::2]), _pad_lanes(gdn_a[..., 1::2]),
                               _pad_lanes(gdn_b[..., 0::2]), _pad_lanes(gdn_b[..., 1::2])], axis=-1).astype(BF16)
    conv_w = jnp.concatenate([gdn_conv_w, ssd_conv_w], axis=-1)
    conv_b = jnp.concatenate([jnp.zeros((depth, GDN_QKV), F32), ssd_conv_b], axis=-1).reshape(depth, 1, CONV_N)
    ssd_dtb = _pad_lanes(ssd_dt_bias)
    ssd_alog = _pad_lanes(ssd_a_log)
    ssd_dskip = jnp.repeat(ssd_d, SSD_HEAD_DIM, axis=-1)
    gdn_dtb = jnp.stack([_pad_lanes(gdn_dt_bias[:, 0::2]), _pad_lanes(gdn_dt_bias[:, 1::2])], axis=1)
    gdn_alog = jnp.stack([_pad_lanes(gdn_a_log[:, 0::2]), _pad_lanes(gdn_a_log[:, 1::2])], axis=1)
    f1g, f1u, f1d = ffn1_w_gate.astype(BF16), ffn1_w_up.astype(BF16), ffn1_w_down.astype(BF16)
    f2g, f2u, f2d = ffn2_w_gate.astype(BF16), ffn2_w_up.astype(BF16), ffn2_w_down.astype(BF16)
    wo_ssd, wo_gdn, wo_attn, wo = (w_o_ssd.astype(BF16), w_o_gdn.astype(BF16), w_o_attn.astype(BF16),
                                   w_out.astype(BF16))
    bias_tiles = _bias_tiles(rel_bias, t)

    mod = _mod_all(c, w_mod, b_mod).reshape(depth, N_MOD, d)
    h = x.reshape(t, d)
    for l in range(depth):
        sh1, sc1, g1, sh2, sc2, g2, sh3, sc3, g3 = [mod[l, i][None, :] for i in range(N_MOD)]
        h = _ffn(h, norm_ffn1[l][None, :], sh1, sc1, g1, f1g, f1u, f1d, l)
        u = _modulate(h, norm_mix[l][None, :], sh2, sc2)
        conv = _matmul_conv(u, w_in_t, conv_w, conv_b, l, conv_segments)
        plain = _matmul_t(u, w_in_t, l, plain_segments)
        small = _matmul(u, w_small, l, tn=SMALL_N)
        y_ssd = _ssd(conv, plain, small, ssd_dtb[l][None, :], ssd_alog[l][None, :], ssd_dskip[l][None, :],
                     ssd_norm[l][None, :])
        y_gdn = _gdn(conv, plain, small, gdn_dtb[l], gdn_alog[l], gdn_norm[l][None, :])
        y_attn = _attn(plain, attn_q_norm[l][None, :], attn_k_norm[l][None, :], bias_tiles)
        merged = _merge(y_ssd, y_gdn, y_attn, wo_ssd, wo_gdn, wo_attn, plain, l)
        h = _out_proj(merged, wo, h, g2, l)
        h = _ffn(h, norm_ffn2[l][None, :], sh3, sc3, g3, f2g, f2u, f2d, l)
    return h.reshape(bsz, t, d)
```

```python
import functools
import math

import jax
import jax.numpy as jnp
from jax import lax
from jax.experimental import pallas as pl
from jax.experimental.pallas import tpu as pltpu

F32 = jnp.float32
BF16 = jnp.bfloat16

D_MODEL = 2048
DEPTH = 4
D_FF = 5632
CONV_K = 4
SSD_HEADS = 32
SSD_HEAD_DIM = 64
SSD_D_INNER = SSD_HEADS * SSD_HEAD_DIM
SSD_GROUPS = 4
SSD_STATE = 128
SSD_CHUNK = 128
GDN_HEADS = 16
GDN_DK = 128
GDN_DV = 128
GDN_CHUNK = 64
ATTN_HEADS = 16
ATTN_HEAD_DIM = 128
MOBA_BLOCK = 256
MOBA_TOPK = 3
REL_BUCKETS = 32
REL_MAX_DIST = 4096
N_BRANCHES = 3
N_MOD = 9
EPS = 1e-6
NEG_INF = -1e30

SSD_XBC = SSD_D_INNER + 2 * SSD_GROUPS * SSD_STATE
GDN_QKV = GDN_HEADS * (2 * GDN_DK + GDN_DV)
ATTN_QKV = 3 * ATTN_HEADS * ATTN_HEAD_DIM
IN_SPLITS = (SSD_D_INNER, SSD_XBC, SSD_HEADS, GDN_QKV, GDN_HEADS * GDN_DV, GDN_HEADS, GDN_HEADS, ATTN_QKV,
             N_BRANCHES * D_MODEL)

LANE = 128
V7X_VMEM_BYTES = 64 * 1024 * 1024
N_BIAS_TILES = 15
ATTN_UNROLL = 4
ATTN_QK_BLOCKS = 4
GDN_CHUNKS_PER_STEP = 2
ATTN_VT_ROWS = ATTN_HEAD_DIM + 16
LOG2E = 1.4426950408889634
CONV_COL_CHUNK = 256
CONV_N = GDN_QKV + SSD_XBC
PLAIN_N = SSD_D_INNER + GDN_HEADS * GDN_DV + ATTN_QKV + N_BRANCHES * D_MODEL
SMALL_N = 5 * LANE


def _params(semantics, vmem_mb):
    return pltpu.CompilerParams(dimension_semantics=semantics, vmem_limit_bytes=vmem_mb * 1024 * 1024)


def _sigmoid(x):
    return 1.0 / (1.0 + jnp.exp(-x))


def _softplus(x):
    return jnp.maximum(x, 0.0) + jnp.log1p(jnp.exp(-jnp.abs(x)))


def _dot(a, b):
    return jnp.dot(a, b, preferred_element_type=F32)


def _dot_nt(a, b):
    return lax.dot_general(a, b, (((1,), (1,)), ((), ())), preferred_element_type=F32)


def _dot_tn(a, b):
    return lax.dot_general(a, b, (((0,), (0,)), ((), ())), preferred_element_type=F32)


def _dot_hi(a, b):
    return jnp.dot(a, b, preferred_element_type=F32, precision=lax.Precision.HIGHEST)


def _mod_kernel(cb_ref, w_ref, b_ref, o_ref):
    cb = cb_ref[...]
    tn = w_ref.shape[-1]
    for j in range(tn // LANE):
        sl = slice(j * LANE, (j + 1) * LANE)
        o_ref[:, sl] = jnp.sum(w_ref[:, sl] * cb, axis=0, keepdims=True) + b_ref[:, sl]


def _mod_all(c, w_mod, b_mod):
    depth, d, n = w_mod.shape
    tn = 1024
    cb = jnp.broadcast_to(c.reshape(d, 1), (d, LANE))
    return pl.pallas_call(
        _mod_kernel,
        grid=(depth, n // tn),
        in_specs=[pl.BlockSpec((d, LANE), lambda l, j: (0, 0)),
                  pl.BlockSpec((None, d, tn), lambda l, j: (l, 0, j)),
                  pl.BlockSpec((None, 1, tn), lambda l, j: (l, 0, j))],
        out_specs=pl.BlockSpec((None, 1, tn), lambda l, j: (l, 0, j)),
        out_shape=jax.ShapeDtypeStruct((depth, 1, n), F32),
        compiler_params=_params(("parallel", "parallel"), 40),
        name="adaln_mod",
    )(cb, w_mod, b_mod.reshape(depth, 1, n))


def _modulated(x, gn, sh, sc):
    xn = x * lax.rsqrt(jnp.mean(x * x, axis=-1, keepdims=True) + EPS)
    return (xn * gn) * (1.0 + sc) + sh


def _modulate_kernel(h_ref, gn_ref, sh_ref, sc_ref, o_ref):
    o_ref[...] = _modulated(h_ref[...], gn_ref[...], sh_ref[...], sc_ref[...]).astype(o_ref.dtype)


def _modulate(h, gn, sh, sc):
    t, d = h.shape
    tm = 512
    vec = pl.BlockSpec((1, d), lambda i: (0, 0))
    return pl.pallas_call(
        _modulate_kernel,
        grid=(t // tm,),
        in_specs=[pl.BlockSpec((tm, d), lambda i: (i, 0)), vec, vec, vec],
        out_specs=pl.BlockSpec((tm, d), lambda i: (i, 0)),
        out_shape=jax.ShapeDtypeStruct((t, d), BF16),
        compiler_params=_params(("parallel",), 32),
        name="modulate",
    )(h, gn, sh, sc)


def _ffn_kernel(h_ref, gn_ref, sh_ref, sc_ref, g_ref, wg_ref, wu_ref, wd_ref, o_ref, xm_ref):
    f = pl.program_id(1)

    @pl.when(f == 0)
    def _():
        xm_ref[...] = _modulated(h_ref[...], gn_ref[...], sh_ref[...], sc_ref[...]).astype(BF16)
        o_ref[...] = jnp.zeros_like(o_ref)

    xm = xm_ref[...]
    a = _dot(xm, wg_ref[...].astype(BF16))
    b = _dot(xm, wu_ref[...].astype(BF16))
    hm = ((a * _sigmoid(a)) * b).astype(BF16)
    o_ref[...] += _dot(hm, wd_ref[...].astype(BF16))

    @pl.when(f == pl.num_programs(1) - 1)
    def _():
        o_ref[...] = h_ref[...] + (0.5 * g_ref[...]) * o_ref[...]


def _ffn(h, gn, sh, sc, g, wg, wu, wd, l):
    t, d = h.shape
    ff = wg.shape[-1]
    tm, tf = 1024, 256
    vec = pl.BlockSpec((1, d), lambda i, f: (0, 0))
    return pl.pallas_call(
        _ffn_kernel,
        grid=(t // tm, ff // tf),
        in_specs=[pl.BlockSpec((tm, d), lambda i, f: (i, 0)), vec, vec, vec, vec,
                  pl.BlockSpec((None, d, tf), lambda i, f: (l, 0, f)),
                  pl.BlockSpec((None, d, tf), lambda i, f: (l, 0, f)),
                  pl.BlockSpec((None, tf, d), lambda i, f: (l, f, 0))],
        out_specs=pl.BlockSpec((tm, d), lambda i, f: (i, 0)),
        out_shape=jax.ShapeDtypeStruct((t, d), F32),
        scratch_shapes=[pltpu.VMEM((tm, d), BF16)],
        compiler_params=_params(("parallel", "arbitrary"), 58),
        name="ffn",
    )(h, gn, sh, sc, g, wg, wu, wd)


def _mm_kernel(x_ref, w_ref, o_ref):
    o_ref[...] = _dot(x_ref[...], w_ref[...]).astype(o_ref.dtype)


def _matmul(x, w, l, tn, tm=1024, out_dtype=F32):
    t, k = x.shape
    n = w.shape[-1]
    return pl.pallas_call(
        _mm_kernel,
        grid=(n // tn, t // tm),
        in_specs=[pl.BlockSpec((tm, k), lambda j, i: (i, 0)),
                  pl.BlockSpec((None, k, tn), lambda j, i: (l, 0, j))],
        out_specs=pl.BlockSpec((tm, tn), lambda j, i: (i, j)),
        out_shape=jax.ShapeDtypeStruct((t, n), out_dtype),
        compiler_params=_params(("parallel", "parallel"), 48),
        name="proj",
    )(x, w)


def _segment_row(j, tn, segments):
    col = j * tn
    row = col
    out_lo = 0
    for src, size in segments:
        assert size % tn == 0 and src % 8 == 0
        row = jnp.where(col >= out_lo, src + (col - out_lo), row)
        out_lo += size
    return pl.multiple_of(row, 8)


def _wt_spec(l, tn, k, segments):
    return pl.BlockSpec((pl.Element(1), pl.Element(tn), pl.Element(k)),
                        lambda j, i: (l, _segment_row(j, tn, segments), 0))


def _mm_t_kernel(x_ref, wt_ref, o_ref, wb_ref):
    @pl.when(pl.program_id(1) == 0)
    def _():
        wb_ref[...] = wt_ref[0].astype(BF16)

    o_ref[...] = _dot_nt(x_ref[...], wb_ref[...]).astype(o_ref.dtype)


def _matmul_t(x, w_t, l, segments, tn=1024, tm=1024):
    t, k = x.shape
    n = sum(size for _, size in segments)
    return pl.pallas_call(
        _mm_t_kernel,
        grid=(n // tn, t // tm),
        in_specs=[pl.BlockSpec((tm, k), lambda j, i: (i, 0)), _wt_spec(l, tn, k, segments)],
        out_specs=pl.BlockSpec((tm, tn), lambda j, i: (i, j)),
        out_shape=jax.ShapeDtypeStruct((t, n), F32),
        scratch_shapes=[pltpu.VMEM((tn, k), BF16)],
        compiler_params=_params(("parallel", "arbitrary"), 48),
        name="proj",
    )(x, w_t)


def _mm_conv_kernel(x_ref, wt_ref, cw_ref, cb_ref, o_ref, win_ref, wb_ref):
    i = pl.program_id(1)

    @pl.when(i == 0)
    def _():
        win_ref[:, 0:8, :] = jnp.zeros((win_ref.shape[0], 8, win_ref.shape[2]), F32)
        wb_ref[...] = wt_ref[0].astype(BF16)

    tm, tn = o_ref.shape
    cc = CONV_COL_CHUNK
    n_chunks = tn // cc

    def project(c):
        hm = tm // 2
        for r in range(2):
            win_ref[c, 8 + r * hm:8 + (r + 1) * hm, :] = _dot_nt(x_ref[r * hm:(r + 1) * hm, :],
                                                                wb_ref[c * cc:(c + 1) * cc, :])

    def conv(c):
        cs = slice(c * cc, (c + 1) * cc)
        cw = cw_ref[:, cs]
        y = cb_ref[:, cs] + cw[CONV_K - 1:CONV_K, :] * win_ref[c, 8:8 + tm, :]
        for s in range(1, CONV_K):
            y = y + cw[CONV_K - 1 - s:CONV_K - s, :] * win_ref[c, 8 - s:8 - s + tm, :]
        o_ref[:, cs] = y * _sigmoid(y)
        win_ref[c, 0:8, :] = win_ref[c, tm:tm + 8, :]

    project(0)
    for c in range(1, n_chunks):
        project(c)
        conv(c - 1)
    conv(n_chunks - 1)


def _matmul_conv(x, w_t, cw, cb, l, segments, tn=1024, tm=1024):
    t, k = x.shape
    n = sum(size for _, size in segments)
    return pl.pallas_call(
        _mm_conv_kernel,
        grid=(n // tn, t // tm),
        in_specs=[pl.BlockSpec((tm, k), lambda j, i: (i, 0)),
                  _wt_spec(l, tn, k, segments),
                  pl.BlockSpec((None, CONV_K, tn), lambda j, i: (l, 0, j)),
                  pl.BlockSpec((None, 1, tn), lambda j, i: (l, 0, j))],
        out_specs=pl.BlockSpec((tm, tn), lambda j, i: (i, j)),
        out_shape=jax.ShapeDtypeStruct((t, n), F32),
        scratch_shapes=[pltpu.VMEM((tn // CONV_COL_CHUNK, tm + 8, CONV_COL_CHUNK), F32),
                        pltpu.VMEM((tn, k), BF16)],
        compiler_params=_params(("parallel", "arbitrary"), 56),
        name="proj_conv",
    )(x, w_t, cw, cb)


def _ssd_kernel(xbc_ref, z_ref, dt_ref, dtb_ref, alog_ref, dskip_ref, ng_ref, o_ref, state_ref):
    c = pl.program_id(0)

    @pl.when(c == 0)
    def _():
        state_ref[...] = jnp.zeros_like(state_ref)

    L = SSD_CHUNK
    P = SSD_HEAD_DIM
    gw = SSD_D_INNER // SSD_GROUPS
    hpg = SSD_HEADS // SSD_GROUPS
    dt = _softplus(dt_ref[...] + dtb_ref[...])
    a = -jnp.exp(alog_ref[...])
    da = dt * a
    r_i = lax.broadcasted_iota(jnp.int32, (L, L), 0)
    c_i = lax.broadcasted_iota(jnp.int32, (L, L), 1)
    tri = r_i >= c_i
    acs = _dot_hi(tri.astype(F32), da)
    acs_t = acs.T
    last = acs[L - 1:L, :]
    e_acs = jnp.exp(acs)
    e_dec = jnp.exp(last - acs)
    e_last = jnp.exp(last)
    lane = lax.broadcasted_iota(jnp.int32, (L, LANE), 1)
    lo_half = lane < P

    def pair_bcast(v, h):
        rows = v.shape[0]
        return jnp.where(lo_half[:rows], jnp.broadcast_to(v[:, h:h + 1], (rows, LANE)),
                         jnp.broadcast_to(v[:, h + 1:h + 2], (rows, LANE)))

    for g in range(SSD_GROUPS):
        bg = xbc_ref[:, SSD_D_INNER + g * SSD_STATE:SSD_D_INNER + (g + 1) * SSD_STATE].astype(BF16)
        cg = xbc_ref[:, SSD_D_INNER + SSD_GROUPS * SSD_STATE + g * SSD_STATE:
                     SSD_D_INNER + SSD_GROUPS * SSD_STATE + (g + 1) * SSD_STATE].astype(BF16)
        cb = _dot_nt(cg, bg)
        s_in = state_ref[g]
        y_off = _dot(cg, s_in.astype(BF16))
        xdec_parts = []
        dec_parts = []
        y_parts = []
        for hp in range(hpg // 2):
            h = g * hpg + 2 * hp
            ch = slice(h * P, (h + 2) * P)
            xs = xbc_ref[:, ch]
            xdt = xs * pair_bcast(dt, h)
            gmats = []
            for hh in (h, h + 1):
                diff = acs[:, hh:hh + 1] - acs_t[hh:hh + 1, :]
                lm = jnp.exp(jnp.where(tri, diff, -jnp.inf))
                gmats.append((lm * cb).astype(BF16))
            lhs = jnp.concatenate(gmats, axis=1)
            rhs = jnp.concatenate([jnp.where(lo_half, xdt, 0.0), jnp.where(lo_half, 0.0, xdt)], axis=0).astype(BF16)
            y = _dot(lhs, rhs)
            y = y + y_off[:, 2 * hp * P:(2 * hp + 2) * P] * pair_bcast(e_acs, h)
            y = y + xs * dskip_ref[:, ch]
            zz = z_ref[:, ch]
            y_parts.append(y * (zz * _sigmoid(zz)))
            xdec_parts.append((xdt * pair_bcast(e_dec, h)).astype(BF16))
            dec_parts.append(pair_bcast(e_last, h))
        xdec = jnp.concatenate(xdec_parts, axis=1)
        dec = jnp.concatenate(dec_parts, axis=1)
        state_ref[g] = s_in * dec + _dot_tn(bg, xdec)
        ssq = sum(jnp.sum(y * y, axis=-1, keepdims=True) for y in y_parts)
        inv_rms = lax.rsqrt(ssq * (1.0 / gw) + EPS)
        for i, y in enumerate(y_parts):
            ch = slice(g * gw + i * LANE, g * gw + (i + 1) * LANE)
            o_ref[:, ch] = (y * inv_rms * ng_ref[:, ch]).astype(o_ref.dtype)


def _ssd(conv, plain, small, dtb, alog, dskip, ng):
    t = conv.shape[0]
    L = SSD_CHUNK
    xbc_blk = GDN_QKV // SSD_XBC
    vec = lambda n: pl.BlockSpec((1, n), lambda c: (0, 0))
    return pl.pallas_call(
        _ssd_kernel,
        grid=(t // L,),
        in_specs=[pl.BlockSpec((L, SSD_XBC), lambda c: (c, xbc_blk)),
                  pl.BlockSpec((L, SSD_D_INNER), lambda c: (c, 0)),
                  pl.BlockSpec((L, LANE), lambda c: (c, 0)),
                  vec(LANE), vec(LANE), vec(SSD_D_INNER), vec(SSD_D_INNER)],
        out_specs=pl.BlockSpec((L, SSD_D_INNER), lambda c: (c, 0)),
        out_shape=jax.ShapeDtypeStruct((t, SSD_D_INNER), BF16),
        scratch_shapes=[pltpu.VMEM((SSD_GROUPS, SSD_STATE, SSD_D_INNER // SSD_GROUPS), F32)],
        compiler_params=_params(("arbitrary",), 32),
        name="ssd",
    )(conv, plain, small, dtb, alog, dskip, ng)


def _gdn_kernel(qkv_ref, z_ref, ae_ref, ao_ref, be_ref, bo_ref, dtb_ref, alog_ref, ng_ref, o_ref, state_ref):
    c = pl.program_id(0)

    @pl.when(c == 0)
    def _():
        state_ref[...] = jnp.zeros_like(state_ref)

    L = GDN_CHUNK
    R = 2 * L
    n_pairs = GDN_HEADS // 2
    n_ch = qkv_ref.shape[0] // L
    row1 = lax.broadcasted_iota(jnp.int32, (R, LANE), 0)
    top = row1 < L
    dtb2 = jnp.where(top, dtb_ref[0:1, :], dtb_ref[1:2, :])
    alog2 = jnp.where(top, alog_ref[0:1, :], alog_ref[1:2, :])
    r_i = lax.broadcasted_iota(jnp.int32, (R, R), 0)
    c_i = lax.broadcasted_iota(jnp.int32, (R, R), 1)
    same = (r_i < L) == (c_i < L)
    incl = same & (r_i >= c_i)
    strict = same & (r_i > c_i)
    incl_f = incl.astype(F32)
    eye = (r_i == c_i).astype(F32)
    merge_masks = []
    for lb in range(int(math.log2(L))):
        r_blk = jnp.right_shift(r_i, lb)
        c_blk = jnp.right_shift(c_i, lb)
        merge_masks.append((r_blk == c_blk + 1) & (jnp.bitwise_and(r_blk, 1) == 1))
    row2 =lax.broadcasted_iota(jnp.int32, (R, 2 * LANE), 0)
    col2 = lax.broadcasted_iota(jnp.int32, (R, 2 * LANE), 1)
    diag2 = (row2 < L) == (col2 < LANE)
    srow = lax.broadcasted_iota(jnp.int32, (2 * GDN_DK, LANE), 0)

    def stack(ref, base, p, ch):
        lo = base + 2 * p * LANE
        rows = slice(ch * L, (ch + 1) * L)
        return jnp.concatenate([ref[rows, lo:lo + LANE], ref[rows, lo + LANE:lo + 2 * LANE]], axis=0)

    def blockdiag(x):
        return jnp.where(diag2, jnp.concatenate([x, x], axis=1), 0.0).astype(BF16)

    def col(v, p):
        return jnp.broadcast_to(v[:, p:p + 1], (R, LANE))

    pairs = range(n_pairs)
    units = [(ch, p) for ch in range(n_ch) for p in pairs]
    a_b, qk_b, rhs_b, q2, k2, e_gl = {}, {}, {}, {}, {}, {}
    for ch in range(n_ch):
        rows = slice(ch * L, (ch + 1) * L)
        a2 = jnp.concatenate([ae_ref[rows, :], ao_ref[rows, :]], axis=0)
        b2 = jnp.concatenate([be_ref[rows, :], bo_ref[rows, :]], axis=0)
        beta2 = _sigmoid(b2)
        g2 = -jnp.exp(alog2) * _softplus(a2 + dtb2)
        gc2 = _dot_hi(incl_f, g2)
        gc2_t = gc2.T
        gl2 = jnp.where(top, gc2[L - 1:L, :], gc2[R - 1:R, :])
        e_gc = jnp.exp(gc2)
        e_rem = jnp.exp(gl2 - gc2)
        e_gl[ch] = jnp.exp(gl2)
        for p in pairs:
            u = (ch, p)
            q = stack(qkv_ref, 0, p, ch)
            k = stack(qkv_ref, GDN_HEADS * GDN_DK, p, ch)
            v = stack(qkv_ref, 2 * GDN_HEADS * GDN_DK, p, ch)
            qn = q * lax.rsqrt(jnp.sum(q * q, axis=-1, keepdims=True) + EPS) * (GDN_DK ** -0.5)
            kn = k * lax.rsqrt(jnp.sum(k * k, axis=-1, keepdims=True) + EPS)
            beta = col(beta2, p)
            diff = gc2[:, p:p + 1] - gc2_t[p:p + 1, :]
            decay = jnp.exp(jnp.where(incl, diff, -jnp.inf))
            kb = kn * beta
            kn_b = kn.astype(BF16)
            a_b[u] = jnp.where(strict, _dot_nt(kb.astype(BF16), kn_b) * decay, 0.0).astype(BF16)
            qk_b[u] = jnp.where(incl, _dot_nt(qn.astype(BF16), kn_b) * decay, 0.0).astype(BF16)
            rhs_b[u] = jnp.concatenate([v * beta, kb * col(e_gc, p)], axis=1).astype(BF16)
            q2[u] = blockdiag(qn * col(e_gc, p))
            k2[u] = blockdiag(kn * col(e_rem, p))
    zero_b = jnp.zeros((R, R), BF16)
    inv = {u: eye - jnp.where(merge_masks[0], a_b[u], zero_b).astype(F32) for u in units}
    for mk in merge_masks[1:]:
        inv_b = {u: inv[u].astype(BF16) for u in units}
        de = {u: _dot(inv_b[u], jnp.where(mk, a_b[u], zero_b)).astype(BF16) for u in units}
        inv = {u: inv[u] - _dot(de[u], inv_b[u]) for u in units}
    sol = {u: _dot(inv[u].astype(BF16), rhs_b[u]) for u in units}
    s = [state_ref[p] for p in pairs]
    o = {}
    for ch in range(n_ch):
        s_b = [s[p].astype(BF16) for p in pairs]
        v_cb = [(sol[ch, p][:, :GDN_DV] - _dot(blockdiag(sol[ch, p][:, GDN_DV:]), s_b[p])).astype(BF16)
                for p in pairs]
        for p in pairs:
            o[ch, p] = _dot(q2[ch, p], s_b[p]) + _dot(qk_b[ch, p], v_cb[p])
        for p in pairs:
            sdec = jnp.where(srow < GDN_DK, jnp.broadcast_to(e_gl[ch][0:1, p:p + 1], (2 * GDN_DK, LANE)),
                             jnp.broadcast_to(e_gl[ch][R - 1:R, p:p + 1], (2 * GDN_DK, LANE)))
            s[p] = s[p] * sdec + _dot_tn(k2[ch, p], v_cb[p])
    for p in pairs:
        state_ref[p] = s[p]
    for ch, p in units:
        z = stack(z_ref, 0, p, ch)
        on = o[ch, p] * lax.rsqrt(jnp.mean(o[ch, p] * o[ch, p], axis=-1, keepdims=True) + EPS) * ng_ref[...]
        res = (on * (z * _sigmoid(z))).astype(o_ref.dtype)
        rows = slice(ch * L, (ch + 1) * L)
        o_ref[rows, 2 * p * LANE:(2 * p + 1) * LANE] = res[:L]
        o_ref[rows, (2 * p + 1) * LANE:(2 * p + 2) * LANE] = res[L:]


def _gdn(conv, plain, small, dtb, alog, ng):
    t = conv.shape[0]
    L = GDN_CHUNKS_PER_STEP * GDN_CHUNK
    dz = GDN_HEADS * GDN_DV
    sm = lambda j: pl.BlockSpec((L, LANE), lambda c: (c, j))
    return pl.pallas_call(
        _gdn_kernel,
        grid=(t // L,),
        in_specs=[pl.BlockSpec((L, GDN_QKV), lambda c: (c, 0)),
                  pl.BlockSpec((L, dz), lambda c: (c, SSD_D_INNER // dz)),
                  sm(1), sm(2), sm(3), sm(4),
                  pl.BlockSpec((2, LANE), lambda c: (0, 0)),
                  pl.BlockSpec((2, LANE), lambda c: (0, 0)),
                  pl.BlockSpec((1, GDN_DV), lambda c: (0, 0))],
        out_specs=pl.BlockSpec((L, dz), lambda c: (c, 0)),
        out_shape=jax.ShapeDtypeStruct((t, dz), BF16),
        scratch_shapes=[pltpu.VMEM((GDN_HEADS // 2, 2 * GDN_DK, GDN_DV), F32)],
        compiler_params=_params(("arbitrary",), 40),
        name="gdn",
    )(conv, plain, small, small, small, small, dtb, alog, ng)


def _attn_kernel(q_ref, qnext_ref, k_ref, v_ref, qg_ref, kg_ref, bias_ref, o_ref,
                 kaug_ref, vaug_ref, kmean_ref, s_ref, qaug_ref):
    qt = pl.program_id(1)
    B = MOBA_BLOCK
    t = k_ref.shape[0]
    nb = t // B

    def query_operand(q, own):
        qn = q * lax.rsqrt(jnp.mean(q * q, axis=-1, keepdims=True) + EPS) * qg_ref[...]
        gate = lax.dot_general(kmean_ref[...], qn, (((1,), (1,)), ((), ())), preferred_element_type=F32,
                               precision=lax.Precision.HIGHEST)
        blk = lax.broadcasted_iota(jnp.int32, gate.shape, 0)
        past = blk < own
        gate = jnp.where(past, gate, NEG_INF)
        chosen = blk == own
        blk_f = blk.astype(F32)
        for _ in range(MOBA_TOPK):
            best = jnp.max(gate, axis=0, keepdims=True)
            first = jnp.min(jnp.where(gate == best, blk_f, float(nb)), axis=0, keepdims=True)
            hit = blk_f == first
            chosen = chosen | (hit & past)
            gate = jnp.where(hit, -jnp.inf, gate)
        selb_t = jnp.where(chosen, 0.0, NEG_INF)
        selb_t = jnp.concatenate([selb_t, jnp.zeros((LANE - nb, B), F32)], axis=0)
        q_t = (qn * (ATTN_HEAD_DIM ** -0.5 * LOG2E)).T
        return jnp.concatenate([q_t.astype(BF16), selb_t.astype(BF16)], axis=0)

    @pl.when(qt == 0)
    def _():
        lane_b = lax.broadcasted_iota(jnp.int32, (B, LANE), 1)
        vaug_ref[ATTN_HEAD_DIM:ATTN_VT_ROWS, :] = jnp.ones((ATTN_VT_ROWS - ATTN_HEAD_DIM, t), BF16)
        for b in range(nb):
            rows = slice(b * B, (b + 1) * B)
            k = k_ref[rows, :]
            kn = k * lax.rsqrt(jnp.mean(k * k, axis=-1, keepdims=True) + EPS) * kg_ref[...]
            kaug_ref[rows, 0:LANE] = kn.astype(BF16)
            kaug_ref[rows, LANE:2 * LANE] = jnp.where(lane_b == b, 1.0, 0.0).astype(BF16)
            vaug_ref[0:ATTN_HEAD_DIM, rows] = v_ref[rows, :].T.astype(BF16)
            kmean_ref[b:b + 1, :] = jnp.mean(kn, axis=0, keepdims=True)
        qaug_ref[0] = query_operand(q_ref[...], 0)

    own = qt
    slot = lax.rem(qt, 2)

    def attend(n_blocks):
        q_aug_t = qaug_ref[slot]
        m = jnp.full((1, B), NEG_INF, F32)
        for j0 in range(0, n_blocks, ATTN_QK_BLOCKS):
            s_all = _dot(kaug_ref[j0 * B:(j0 + ATTN_QK_BLOCKS) * B, :], q_aug_t)
            for u in range(ATTN_QK_BLOCKS):
                j = j0 + u
                tile = jnp.where(j > own, N_BIAS_TILES - 1, jnp.minimum(own - j, N_BIAS_TILES - 2))
                s = s_all[u * B:(u + 1) * B] + bias_ref[tile]
                s_ref[j] = s
                m = jnp.maximum(m, jnp.max(s, axis=0, keepdims=True))
        qaug_ref[1 - slot] = query_operand(qnext_ref[...], qt + 1)
        acc = jnp.zeros((ATTN_VT_ROWS, B), F32)
        for j in range(n_blocks):
            p = jnp.exp2(s_ref[j] - m)
            acc = acc + _dot(vaug_ref[:, j * B:(j + 1) * B], p.astype(BF16))
        out_t = acc[:ATTN_HEAD_DIM] / acc[ATTN_HEAD_DIM:ATTN_HEAD_DIM + 1]
        o_ref[...] = out_t.T.astype(o_ref.dtype)

    n_branches = -(-nb // ATTN_UNROLL)
    for i in range(n_branches):
        @pl.when(own // ATTN_UNROLL == i)
        def _(i=i):
            attend(min((i + 1) * ATTN_UNROLL, nb))


def _attn(plain, qg, kg, bias_tiles):
    t = plain.shape[0]
    B = MOBA_BLOCK
    hd = ATTN_HEAD_DIM
    base = (SSD_D_INNER + GDN_HEADS * GDN_DV) // hd
    return pl.pallas_call(
        _attn_kernel,
        grid=(ATTN_HEADS, t // B),
        in_specs=[pl.BlockSpec((B, hd), lambda h, i: (i, base + h)),
                  pl.BlockSpec((B, hd), lambda h, i: (jnp.minimum(i + 1, t // B - 1), base + h)),
                  pl.BlockSpec((t, hd), lambda h, i: (0, base + ATTN_HEADS + h)),
                  pl.BlockSpec((t, hd), lambda h, i: (0, base + 2 * ATTN_HEADS + h)),
                  pl.BlockSpec((1, hd), lambda h, i: (0, 0)),
                  pl.BlockSpec((1, hd), lambda h, i: (0, 0)),
                  pl.BlockSpec((None, N_BIAS_TILES, B, B), lambda h, i: (h, 0, 0, 0))],
        out_specs=pl.BlockSpec((B, hd), lambda h, i: (i, h)),
        out_shape=jax.ShapeDtypeStruct((t, ATTN_HEADS * hd), BF16),
        scratch_shapes=[pltpu.VMEM((t, 2 * LANE), BF16), pltpu.VMEM((ATTN_VT_ROWS, t), BF16),
                        pltpu.VMEM((t // B, hd), F32), pltpu.VMEM((t // B, B, B), F32),
                        pltpu.VMEM((2, 2 * LANE, B), BF16)],
        compiler_params=_params(("arbitrary", "arbitrary"), 56),
        name="moba",
    )(plain, plain, plain, plain, qg, kg, bias_tiles)


def _bias_tiles(rel_bias, t):
    B = MOBA_BLOCK
    dist = jnp.arange(t, dtype=jnp.int32)
    exact = REL_BUCKETS // 2
    nf = jnp.maximum(dist, 1).astype(F32)
    large = exact + (jnp.log(nf / exact) / math.log(REL_MAX_DIST / exact) * (REL_BUCKETS - exact)).astype(jnp.int32)
    bucket = jnp.where(dist < exact, dist, jnp.minimum(large, REL_BUCKETS - 1))
    by_dist = rel_bias.T[:, bucket] * LOG2E
    nh = by_dist.shape[0]
    i = jnp.arange(B)[:, None]
    j = jnp.arange(B)[None, :]
    tiles = []
    for d in range(N_BIAS_TILES - 2):
        idx = jnp.clip(d * B + jnp.arange(2 * B - 1) - (B - 1), 0, t - 1)
        row = jnp.pad(by_dist[:, idx], ((0, 0), (0, 1)))
        skew = jnp.broadcast_to(row[:, None, :], (nh, B, 2 * B)).reshape(nh, 2 * B * B)
        tile = skew[:, :B * (2 * B - 1)].reshape(nh, B, 2 * B - 1)[:, :, B - 1:]
        if d == 0:
            tile = jnp.where(j >= i, tile, NEG_INF)
        tiles.append(tile)
    far = jnp.broadcast_to((rel_bias.T[:, REL_BUCKETS - 1] * LOG2E)[:, None, None], tiles[0].shape)
    tiles.append(far)
    tiles.append(jnp.full_like(far, NEG_INF))
    return jnp.stack(tiles, axis=1)


def _merge_kernel(y1_ref, y2_ref, y3_ref, w1_ref, w2_ref, w3_ref, g1_ref, g2_ref, g3_ref, o_ref):
    acc = _sigmoid(g1_ref[...]) * _dot(y1_ref[...], w1_ref[...])
    acc = acc + _sigmoid(g2_ref[...]) * _dot(y2_ref[...], w2_ref[...])
    acc = acc + _sigmoid(g3_ref[...]) * _dot(y3_ref[...], w3_ref[...])
    o_ref[...] = acc.astype(o_ref.dtype)


def _merge(y_ssd, y_gdn, y_attn, w1, w2, w3, plain, l):
    t, d = y_ssd.shape
    tm, tn = 512, 512
    gbase = (SSD_D_INNER + GDN_HEADS * GDN_DV + ATTN_QKV) // tn
    yspec = pl.BlockSpec((tm, d), lambda i, j: (i, 0))
    wspec = pl.BlockSpec((None, d, tn), lambda i, j: (l, 0, j))
    gspec = lambda b: pl.BlockSpec((tm, tn), lambda i, j: (i, gbase + b * (D_MODEL // tn) + j))
    return pl.pallas_call(
        _merge_kernel,
        grid=(t // tm, D_MODEL // tn),
        in_specs=[yspec, yspec, yspec, wspec, wspec, wspec, gspec(0), gspec(1), gspec(2)],
        out_specs=pl.BlockSpec((tm, tn), lambda i, j: (i, j)),
        out_shape=jax.ShapeDtypeStruct((t, D_MODEL), BF16),
        compiler_params=_params(("parallel", "parallel"), 48),
        name="merge",
    )(y_ssd, y_gdn, y_attn, w1, w2, w3, plain, plain, plain)


def _out_kernel(x_ref, w_ref, h_ref, g_ref, o_ref):
    o_ref[...] = h_ref[...] + g_ref[...] * _dot(x_ref[...], w_ref[...])


def _out_proj(x, w, h, g, l):
    t, k = x.shape
    n = w.shape[-1]
    tm, tn = 1024, 1024
    return pl.pallas_call(
        _out_kernel,
        grid=(n // tn, t // tm),
        in_specs=[pl.BlockSpec((tm, k), lambda j, i: (i, 0)),
                  pl.BlockSpec((None, k, tn), lambda j, i: (l, 0, j)),
                  pl.BlockSpec((tm, tn), lambda j, i: (i, j)),
                  pl.BlockSpec((1, tn), lambda j, i: (0, j))],
        out_specs=pl.BlockSpec((tm, tn), lambda j, i: (i, j)),
        out_shape=jax.ShapeDtypeStruct((t, n), F32),
        compiler_params=_params(("parallel", "parallel"), 48),
        name="out_proj",
    )(x, w, h, g)


def _pad_lanes(w, n=LANE):
    return jnp.pad(w, [(0, 0)] * (w.ndim - 1) + [(0, n - w.shape[-1])])


def _split_in(w_in):
    cuts = []
    lo = 0
    for s in IN_SPLITS:
        cuts.append(w_in[..., lo:lo + s])
        lo += s
    return cuts


def _in_segments():
    names = ("ssd_z", "ssd_xbc", "ssd_dt", "gdn_qkv", "gdn_z", "gdn_a", "gdn_b", "attn_qkv", "gates")
    out, lo = {}, 0
    for name, size in zip(names, IN_SPLITS):
        out[name] = (lo, size)
        lo += size
    return out


def kernel(x, c, w_mod, b_mod, norm_ffn1, ffn1_w_gate, ffn1_w_up, ffn1_w_down, norm_mix, w_in, ssd_conv_w, ssd_conv_b, ssd_dt_bias, ssd_a_log, ssd_d, ssd_norm, w_o_ssd, gdn_conv_w, gdn_dt_bias, gdn_a_log, gdn_norm, w_o_gdn, attn_q_norm, attn_k_norm, rel_bias, w_o_attn, w_out, norm_ffn2, ffn2_w_gate, ffn2_w_up, ffn2_w_down):
    bsz, t, d = x.shape
    assert bsz == 1 and d == D_MODEL and t % 1024 == 0 and t // MOBA_BLOCK <= LANE
    depth = w_mod.shape[0]

    w_in_t = jnp.swapaxes(w_in, 1, 2)
    seg = _in_segments()
    conv_segments = (seg["gdn_qkv"], seg["ssd_xbc"])
    plain_segments = (seg["ssd_z"], seg["gdn_z"], seg["attn_qkv"], seg["gates"])
    _, _, ssd_dt, _, _, gdn_a, gdn_b, _, _ = _split_in(w_in)
    w_small =jnp.concatenate([_pad_lanes(ssd_dt), _pad_lanes(gdn_a[..., 0::2]), _pad_lanes(gdn_a[..., 1::2]),
                               _pad_lanes(gdn_b[..., 0::2]), _pad_lanes(gdn_b[..., 1::2])], axis=-1).astype(BF16)
    conv_w = jnp.concatenate([gdn_conv_w, ssd_conv_w], axis=-1)
    conv_b = jnp.concatenate([jnp.zeros((depth, GDN_QKV), F32), ssd_conv_b], axis=-1).reshape(depth, 1, CONV_N)
    ssd_dtb = _pad_lanes(ssd_dt_bias)
    ssd_alog = _pad_lanes(ssd_a_log)
    ssd_dskip = jnp.repeat(ssd_d, SSD_HEAD_DIM, axis=-1)
    gdn_dtb = jnp.stack([_pad_lanes(gdn_dt_bias[:, 0::2]), _pad_lanes(gdn_dt_bias[:, 1::2])], axis=1)
    gdn_alog = jnp.stack([_pad_lanes(gdn_a_log[:, 0::2]), _pad_lanes(gdn_a_log[:, 1::2])], axis=1)
    f1g, f1u, f1d = ffn1_w_gate, ffn1_w_up, ffn1_w_down
    f2g, f2u, f2d = ffn2_w_gate, ffn2_w_up, ffn2_w_down
    wo_ssd, wo_gdn, wo_attn, wo = (w_o_ssd.astype(BF16), w_o_gdn.astype(BF16), w_o_attn.astype(BF16),
                                   w_out.astype(BF16))
    bias_tiles = _bias_tiles(rel_bias, t)

    mod = _mod_all(c, w_mod, b_mod).reshape(depth, N_MOD, d)
    h = x.reshape(t, d)
    for l in range(depth):
        sh1, sc1, g1, sh2, sc2, g2, sh3, sc3, g3 = [mod[l, i][None, :] for i in range(N_MOD)]
        h = _ffn(h, norm_ffn1[l][None, :], sh1, sc1, g1, f1g, f1u, f1d, l)
        u = _modulate(h, norm_mix[l][None, :], sh2, sc2)
        conv = _matmul_conv(u, w_in_t, conv_w, conv_b, l, conv_segments)
        plain = _matmul_t(u, w_in_t, l, plain_segments)
        small = _matmul(u, w_small, l, tn=SMALL_N)
        y_ssd = _ssd(conv, plain, small, ssd_dtb[l][None, :], ssd_alog[l][None, :], ssd_dskip[l][None, :],
                     ssd_norm[l][None, :])
        y_gdn = _gdn(conv, plain, small, gdn_dtb[l], gdn_alog[l], gdn_norm[l][None, :])
        y_attn = _attn(plain, attn_q_norm[l][None, :], attn_k_norm[l][None, :], bias_tiles)
        merged = _merge(y_ssd, y_gdn, y_attn, wo_ssd, wo_gdn, wo_attn, plain, l)
        h = _out_proj(merged, wo, h, g2, l)
        h = _ffn(h, norm_ffn2[l][None, :], sh3, sc3, g3, f2g, f2u, f2d, l)
    return h.reshape(bsz, t, d)
```

```python
import functools
import math

import jax
import jax.numpy as jnp
from jax import lax
from jax.experimental import pallas as pl
from jax.experimental.pallas import tpu as pltpu

F32 = jnp.float32
BF16 = jnp.bfloat16

D_MODEL = 2048
DEPTH = 4
D_FF = 5632
CONV_K = 4
SSD_HEADS = 32
SSD_HEAD_DIM = 64
SSD_D_INNER = SSD_HEADS * SSD_HEAD_DIM
SSD_GROUPS = 4
SSD_STATE = 128
SSD_CHUNK = 128
GDN_HEADS = 16
GDN_DK = 128
GDN_DV = 128
GDN_CHUNK = 64
ATTN_HEADS = 16
ATTN_HEAD_DIM = 128
MOBA_BLOCK = 256
MOBA_TOPK = 3
REL_BUCKETS = 32
REL_MAX_DIST = 4096
N_BRANCHES = 3
N_MOD = 9
EPS = 1e-6
NEG_INF = -1e30

SSD_XBC = SSD_D_INNER + 2 * SSD_GROUPS * SSD_STATE
GDN_QKV = GDN_HEADS * (2 * GDN_DK + GDN_DV)
ATTN_QKV = 3 * ATTN_HEADS * ATTN_HEAD_DIM
IN_SPLITS = (SSD_D_INNER, SSD_XBC, SSD_HEADS, GDN_QKV, GDN_HEADS * GDN_DV, GDN_HEADS, GDN_HEADS, ATTN_QKV,
             N_BRANCHES * D_MODEL)

LANE = 128
V7X_VMEM_BYTES = 64 * 1024 * 1024
N_BIAS_TILES = 15
ATTN_UNROLL = 2
ATTN_QK_BLOCKS = 2
GDN_CHUNKS_PER_STEP = 2
ATTN_VT_ROWS = ATTN_HEAD_DIM + 16
LOG2E = 1.4426950408889634
CONV_COL_CHUNK = 256
CONV_N = GDN_QKV + SSD_XBC
PLAIN_N = SSD_D_INNER + GDN_HEADS * GDN_DV + ATTN_QKV + N_BRANCHES * D_MODEL
SMALL_N = 5 * LANE


def _params(semantics, vmem_mb):
    return pltpu.CompilerParams(dimension_semantics=semantics, vmem_limit_bytes=vmem_mb * 1024 * 1024)


def _sigmoid(x):
    return 1.0 / (1.0 + jnp.exp(-x))


def _softplus(x):
    return jnp.maximum(x, 0.0) + jnp.log1p(jnp.exp(-jnp.abs(x)))


def _dot(a, b):
    return jnp.dot(a, b, preferred_element_type=F32)


def _dot_nt(a, b):
    return lax.dot_general(a, b, (((1,), (1,)), ((), ())), preferred_element_type=F32)


def _dot_tn(a, b):
    return lax.dot_general(a, b, (((0,), (0,)), ((), ())), preferred_element_type=F32)


def _dot_hi(a, b):
    return jnp.dot(a, b, preferred_element_type=F32, precision=lax.Precision.HIGHEST)


def _mod_kernel(cb_ref, w_ref, b_ref, o_ref):
    cb = cb_ref[...]
    tn = w_ref.shape[-1]
    for j in range(tn // LANE):
        sl = slice(j * LANE, (j + 1) * LANE)
        o_ref[:, sl] = jnp.sum(w_ref[:, sl] * cb, axis=0, keepdims=True) + b_ref[:, sl]


def _mod_all(c, w_mod, b_mod):
    depth, d, n = w_mod.shape
    tn = 1024
    cb = jnp.broadcast_to(c.reshape(d, 1), (d, LANE))
    return pl.pallas_call(
        _mod_kernel,
        grid=(depth, n // tn),
        in_specs=[pl.BlockSpec((d, LANE), lambda l, j: (0, 0)),
                  pl.BlockSpec((None, d, tn), lambda l, j: (l, 0, j)),
                  pl.BlockSpec((None, 1, tn), lambda l, j: (l, 0, j))],
        out_specs=pl.BlockSpec((None, 1, tn), lambda l, j: (l, 0, j)),
        out_shape=jax.ShapeDtypeStruct((depth, 1, n), F32),
        compiler_params=_params(("parallel", "parallel"), 40),
        name="adaln_mod",
    )(cb, w_mod, b_mod.reshape(depth, 1, n))


def _modulated(x, gn, sh, sc):
    xn = x * lax.rsqrt(jnp.mean(x * x, axis=-1, keepdims=True) + EPS)
    return (xn * gn) * (1.0 + sc) + sh


def _modulate_kernel(h_ref, gn_ref, sh_ref, sc_ref, o_ref):
    o_ref[...] = _modulated(h_ref[...], gn_ref[...], sh_ref[...], sc_ref[...]).astype(o_ref.dtype)


def _modulate(h, gn, sh, sc):
    t, d = h.shape
    tm = 512
    vec = pl.BlockSpec((1, d), lambda i: (0, 0))
    return pl.pallas_call(
        _modulate_kernel,
        grid=(t // tm,),
        in_specs=[pl.BlockSpec((tm, d), lambda i: (i, 0)), vec, vec, vec],
        out_specs=pl.BlockSpec((tm, d), lambda i: (i, 0)),
        out_shape=jax.ShapeDtypeStruct((t, d), BF16),
        compiler_params=_params(("parallel",), 32),
        name="modulate",
    )(h, gn, sh, sc)


def _ffn_kernel(h_ref, gn_ref, sh_ref, sc_ref, g_ref, wg_ref, wu_ref, wd_ref, o_ref, xm_ref):
    f = pl.program_id(1)

    @pl.when(f == 0)
    def _():
        xm_ref[...] = _modulated(h_ref[...], gn_ref[...], sh_ref[...], sc_ref[...]).astype(BF16)
        o_ref[...] = jnp.zeros_like(o_ref)

    xm = xm_ref[...]
    a = _dot(xm, wg_ref[...].astype(BF16))
    b = _dot(xm, wu_ref[...].astype(BF16))
    hm = ((a * _sigmoid(a)) * b).astype(BF16)
    o_ref[...] += _dot(hm, wd_ref[...].astype(BF16))

    @pl.when(f == pl.num_programs(1) - 1)
    def _():
        o_ref[...] = h_ref[...] + (0.5 * g_ref[...]) * o_ref[...]


def _ffn(h, gn, sh, sc, g, wg, wu, wd, l):
    t, d = h.shape
    ff = wg.shape[-1]
    tm, tf = 1024, 256
    vec = pl.BlockSpec((1, d), lambda i, f: (0, 0))
    return pl.pallas_call(
        _ffn_kernel,
        grid=(t // tm, ff // tf),
        in_specs=[pl.BlockSpec((tm, d), lambda i, f: (i, 0)), vec, vec, vec, vec,
                  pl.BlockSpec((None, d, tf), lambda i, f: (l, 0, f)),
                  pl.BlockSpec((None, d, tf), lambda i, f: (l, 0, f)),
                  pl.BlockSpec((None, tf, d), lambda i, f: (l, f, 0))],
        out_specs=pl.BlockSpec((tm, d), lambda i, f: (i, 0)),
        out_shape=jax.ShapeDtypeStruct((t, d), F32),
        scratch_shapes=[pltpu.VMEM((tm, d), BF16)],
        compiler_params=_params(("parallel", "arbitrary"), 58),
        name="ffn",
    )(h, gn, sh, sc, g, wg, wu, wd)


def _mm_kernel(x_ref, w_ref, o_ref):
    o_ref[...] = _dot(x_ref[...], w_ref[...]).astype(o_ref.dtype)


def _matmul(x, w, l, tn, tm=1024, out_dtype=F32):
    t, k = x.shape
    n = w.shape[-1]
    return pl.pallas_call(
        _mm_kernel,
        grid=(n // tn, t // tm),
        in_specs=[pl.BlockSpec((tm, k), lambda j, i: (i, 0)),
                  pl.BlockSpec((None, k, tn), lambda j, i: (l, 0, j))],
        out_specs=pl.BlockSpec((tm, tn), lambda j, i: (i, j)),
        out_shape=jax.ShapeDtypeStruct((t, n), out_dtype),
        compiler_params=_params(("parallel", "parallel"), 48),
        name="proj",
    )(x, w)


def _segment_row(j, tn, segments):
    col = j * tn
    row = col
    out_lo = 0
    for src, size in segments:
        assert size % tn == 0 and src % 8 == 0
        row = jnp.where(col >= out_lo, src + (col - out_lo), row)
        out_lo += size
    return pl.multiple_of(row, 8)


def _wt_spec(l, tn, k, segments):
    return pl.BlockSpec((pl.Element(1), pl.Element(tn), pl.Element(k)),
                        lambda j, i: (l, _segment_row(j, tn, segments), 0))


def _mm_t_kernel(x_ref, wt_ref, o_ref, wb_ref):
    @pl.when(pl.program_id(1) == 0)
    def _():
        wb_ref[...] = wt_ref[0].astype(BF16)

    o_ref[...] = _dot_nt(x_ref[...], wb_ref[...]).astype(o_ref.dtype)


def _matmul_t(x, w_t, l, segments, tn=1024, tm=1024):
    t, k = x.shape
    n = sum(size for _, size in segments)
    return pl.pallas_call(
        _mm_t_kernel,
        grid=(n // tn, t // tm),
        in_specs=[pl.BlockSpec((tm, k), lambda j, i: (i, 0)), _wt_spec(l, tn, k, segments)],
        out_specs=pl.BlockSpec((tm, tn), lambda j, i: (i, j)),
        out_shape=jax.ShapeDtypeStruct((t, n), F32),
        scratch_shapes=[pltpu.VMEM((tn, k), BF16)],
        compiler_params=_params(("parallel", "arbitrary"), 48),
        name="proj",
    )(x, w_t)


def _mm_conv_kernel(x_ref, wt_ref, cw_ref, cb_ref, o_ref, win_ref, wb_ref):
    i = pl.program_id(1)

    @pl.when(i == 0)
    def _():
        win_ref[:, 0:8, :] = jnp.zeros((win_ref.shape[0], 8, win_ref.shape[2]), F32)
        wb_ref[...] = wt_ref[0].astype(BF16)

    tm, tn = o_ref.shape
    cc = CONV_COL_CHUNK
    n_chunks = tn // cc

    def project(c):
        hm = tm // 2
        for r in range(2):
            win_ref[c, 8 + r * hm:8 + (r + 1) * hm, :] = _dot_nt(x_ref[r * hm:(r + 1) * hm, :],
                                                                wb_ref[c * cc:(c + 1) * cc, :])

    def conv(c):
        cs = slice(c * cc, (c + 1) * cc)
        cw = cw_ref[:, cs]
        y = cb_ref[:, cs] + cw[CONV_K - 1:CONV_K, :] * win_ref[c, 8:8 + tm, :]
        for s in range(1, CONV_K):
            y = y + cw[CONV_K - 1 - s:CONV_K - s, :] * win_ref[c, 8 - s:8 - s + tm, :]
        o_ref[:, cs] = y * _sigmoid(y)
        win_ref[c, 0:8, :] = win_ref[c, tm:tm + 8, :]

    project(0)
    for c in range(1, n_chunks):
        project(c)
        conv(c - 1)
    conv(n_chunks - 1)


def _matmul_conv(x, w_t, cw, cb, l, segments, tn=1024, tm=1024):
    t, k = x.shape
    n = sum(size for _, size in segments)
    return pl.pallas_call(
        _mm_conv_kernel,
        grid=(n // tn, t // tm),
        in_specs=[pl.BlockSpec((tm, k), lambda j, i: (i, 0)),
                  _wt_spec(l, tn, k, segments),
                  pl.BlockSpec((None, CONV_K, tn), lambda j, i: (l, 0, j)),
                  pl.BlockSpec((None, 1, tn), lambda j, i: (l, 0, j))],
        out_specs=pl.BlockSpec((tm, tn), lambda j, i: (i, j)),
        out_shape=jax.ShapeDtypeStruct((t, n), F32),
        scratch_shapes=[pltpu.VMEM((tn // CONV_COL_CHUNK, tm + 8, CONV_COL_CHUNK), F32),
                        pltpu.VMEM((tn, k), BF16)],
        compiler_params=_params(("parallel", "arbitrary"), 56),
        name="proj_conv",
    )(x, w_t, cw, cb)


def _ssd_kernel(xbc_ref, z_ref, dt_ref, dtb_ref, alog_ref, dskip_ref, ng_ref, o_ref, state_ref):
    c = pl.program_id(0)

    @pl.when(c == 0)
    def _():
        state_ref[...] = jnp.zeros_like(state_ref)

    L = SSD_CHUNK
    P = SSD_HEAD_DIM
    gw = SSD_D_INNER // SSD_GROUPS
    hpg = SSD_HEADS // SSD_GROUPS
    dt = _softplus(dt_ref[...] + dtb_ref[...])
    a = -jnp.exp(alog_ref[...])
    da = dt * a
    r_i = lax.broadcasted_iota(jnp.int32, (L, L), 0)
    c_i = lax.broadcasted_iota(jnp.int32, (L, L), 1)
    tri = r_i >= c_i
    acs = _dot_hi(tri.astype(F32), da)
    acs_t = acs.T
    last = acs[L - 1:L, :]
    e_acs = jnp.exp(acs)
    dt_dec = dt * jnp.exp(last - acs)
    dt_t = dt.T
    e_last = jnp.exp(last)
    lane = lax.broadcasted_iota(jnp.int32, (L, LANE), 1)
    lo_half = lane < P

    def pair_bcast(v, h):
        rows = v.shape[0]
        return jnp.where(lo_half[:rows], jnp.broadcast_to(v[:, h:h + 1], (rows, LANE)),
                         jnp.broadcast_to(v[:, h + 1:h + 2], (rows, LANE)))

    for g in range(SSD_GROUPS):
        bg = xbc_ref[:, SSD_D_INNER + g * SSD_STATE:SSD_D_INNER + (g + 1) * SSD_STATE].astype(BF16)
        cg = xbc_ref[:, SSD_D_INNER + SSD_GROUPS * SSD_STATE + g * SSD_STATE:
                     SSD_D_INNER + SSD_GROUPS * SSD_STATE + (g + 1) * SSD_STATE].astype(BF16)
        cb = _dot_nt(cg, bg)
        s_in = state_ref[g]
        y_off = _dot(cg, s_in.astype(BF16))
        xdec_parts = []
        dec_parts = []
        y_parts = []
        for hp in range(hpg // 2):
            h = g * hpg + 2 * hp
            ch = slice(h * P, (h + 2) * P)
            xs = xbc_ref[:, ch]
            gmats = []
            for hh in (h, h + 1):
                diff = acs[:, hh:hh + 1] - acs_t[hh:hh + 1, :]
                lm = jnp.exp(jnp.where(tri, diff, -jnp.inf))
                gmats.append((lm * cb * dt_t[hh:hh + 1, :]).astype(BF16))
            lhs = jnp.concatenate(gmats, axis=1)
            rhs = jnp.concatenate([jnp.where(lo_half, xs, 0.0), jnp.where(lo_half, 0.0, xs)], axis=0).astype(BF16)
            y = _dot(lhs, rhs)
            y = y + y_off[:, 2 * hp * P:(2 * hp + 2) * P] * pair_bcast(e_acs, h)
            y = y + xs * dskip_ref[:, ch]
            zz = z_ref[:, ch]
            y_parts.append(y * (zz * _sigmoid(zz)))
            xdec_parts.append((xs * pair_bcast(dt_dec, h)).astype(BF16))
            dec_parts.append(pair_bcast(e_last, h))
        xdec = jnp.concatenate(xdec_parts, axis=1)
        dec = jnp.concatenate(dec_parts, axis=1)
        state_ref[g] = s_in * dec + _dot_tn(bg, xdec)
        ssq = sum(jnp.sum(y * y, axis=-1, keepdims=True) for y in y_parts)
        inv_rms = lax.rsqrt(ssq * (1.0 / gw) + EPS)
        for i, y in enumerate(y_parts):
            ch = slice(g * gw + i * LANE, g * gw + (i + 1) * LANE)
            o_ref[:, ch] = (y * inv_rms * ng_ref[:, ch]).astype(o_ref.dtype)


def _ssd(conv, plain, small, dtb, alog, dskip, ng):
    t = conv.shape[0]
    L = SSD_CHUNK
    xbc_blk = GDN_QKV // SSD_XBC
    vec = lambda n: pl.BlockSpec((1, n), lambda c: (0, 0))
    return pl.pallas_call(
        _ssd_kernel,
        grid=(t // L,),
        in_specs=[pl.BlockSpec((L, SSD_XBC), lambda c: (c, xbc_blk)),
                  pl.BlockSpec((L, SSD_D_INNER), lambda c: (c, 0)),
                  pl.BlockSpec((L, LANE), lambda c: (c, 0)),
                  vec(LANE), vec(LANE), vec(SSD_D_INNER), vec(SSD_D_INNER)],
        out_specs=pl.BlockSpec((L, SSD_D_INNER), lambda c: (c, 0)),
        out_shape=jax.ShapeDtypeStruct((t, SSD_D_INNER), BF16),
        scratch_shapes=[pltpu.VMEM((SSD_GROUPS, SSD_STATE, SSD_D_INNER // SSD_GROUPS), F32)],
        compiler_params=_params(("arbitrary",), 32),
        name="ssd",
    )(conv, plain, small, dtb, alog, dskip, ng)


def _gdn_kernel(qkv_ref, z_ref, ae_ref, ao_ref, be_ref, bo_ref, dtb_ref, alog_ref, ng_ref, o_ref, state_ref):
    c = pl.program_id(0)

    @pl.when(c == 0)
    def _():
        state_ref[...] = jnp.zeros_like(state_ref)

    L = GDN_CHUNK
    R = 2 * L
    n_pairs = GDN_HEADS // 2
    n_ch = qkv_ref.shape[0] // L
    row1 = lax.broadcasted_iota(jnp.int32, (R, LANE), 0)
    top = row1 < L
    dtb2 = jnp.where(top, dtb_ref[0:1, :], dtb_ref[1:2, :])
    alog2 = jnp.where(top, alog_ref[0:1, :], alog_ref[1:2, :])
    r_i = lax.broadcasted_iota(jnp.int32, (R, R), 0)
    c_i = lax.broadcasted_iota(jnp.int32, (R, R), 1)
    same = (r_i < L) == (c_i < L)
    incl = same & (r_i >= c_i)
    strict = same & (r_i > c_i)
    incl_f = incl.astype(F32)
    eye = (r_i == c_i).astype(F32)
    merge_masks = []
    for lb in range(int(math.log2(L))):
        r_blk = jnp.right_shift(r_i, lb)
        c_blk = jnp.right_shift(c_i, lb)
        merge_masks.append((r_blk == c_blk + 1) & (jnp.bitwise_and(r_blk, 1) == 1))
    row2 =lax.broadcasted_iota(jnp.int32, (R, 2 * LANE), 0)
    col2 = lax.broadcasted_iota(jnp.int32, (R, 2 * LANE), 1)
    diag2 = (row2 < L) == (col2 < LANE)
    srow = lax.broadcasted_iota(jnp.int32, (2 * GDN_DK, LANE), 0)

    def stack(ref, base, p, ch):
        lo = base + 2 * p * LANE
        rows = slice(ch * L, (ch + 1) * L)
        return jnp.concatenate([ref[rows, lo:lo + LANE], ref[rows, lo + LANE:lo + 2 * LANE]], axis=0)

    def blockdiag(x):
        return jnp.where(diag2, jnp.concatenate([x, x], axis=1), 0.0).astype(BF16)

    def col(v, p):
        return jnp.broadcast_to(v[:, p:p + 1], (R, LANE))

    pairs = range(n_pairs)
    units = [(ch, p) for ch in range(n_ch) for p in pairs]
    a_b, qk_b, rhs_b, q2, k2, e_gl = {}, {}, {}, {}, {}, {}
    for ch in range(n_ch):
        rows = slice(ch * L, (ch + 1) * L)
        a2 = jnp.concatenate([ae_ref[rows, :], ao_ref[rows, :]], axis=0)
        b2 = jnp.concatenate([be_ref[rows, :], bo_ref[rows, :]], axis=0)
        beta2 = _sigmoid(b2)
        g2 = -jnp.exp(alog2) * _softplus(a2 + dtb2)
        gc2 = _dot_hi(incl_f, g2)
        gc2_t = gc2.T
        gl2 = jnp.where(top, gc2[L - 1:L, :], gc2[R - 1:R, :])
        e_gc = jnp.exp(gc2)
        e_rem = jnp.exp(gl2 - gc2)
        e_gl[ch] = jnp.exp(gl2)
        for p in pairs:
            u = (ch, p)
            q = stack(qkv_ref, 0, p, ch)
            k = stack(qkv_ref, GDN_HEADS * GDN_DK, p, ch)
            v = stack(qkv_ref, 2 * GDN_HEADS * GDN_DK, p, ch)
            qn = q * lax.rsqrt(jnp.sum(q * q, axis=-1, keepdims=True) + EPS) * (GDN_DK ** -0.5)
            kn = k * lax.rsqrt(jnp.sum(k * k, axis=-1, keepdims=True) + EPS)
            beta = col(beta2, p)
            diff = gc2[:, p:p + 1] - gc2_t[p:p + 1, :]
            decay = jnp.exp(jnp.where(incl, diff, -jnp.inf))
            kb = kn * beta
            kn_b = kn.astype(BF16)
            a_b[u] = jnp.where(strict, _dot_nt(kb.astype(BF16), kn_b) * decay, 0.0).astype(BF16)
            qk_b[u] = jnp.where(incl, _dot_nt(qn.astype(BF16), kn_b) * decay, 0.0).astype(BF16)
            rhs_b[u] = jnp.concatenate([v * beta, kb * col(e_gc, p)], axis=1).astype(BF16)
            q2[u] = blockdiag(qn * col(e_gc, p))
            k2[u] = blockdiag(kn * col(e_rem, p))
    zero_b = jnp.zeros((R, R), BF16)
    inv = {u: eye - jnp.where(merge_masks[0], a_b[u], zero_b).astype(F32) for u in units}
    for mk in merge_masks[1:]:
        inv_b = {u: inv[u].astype(BF16) for u in units}
        de = {u: _dot(inv_b[u], jnp.where(mk, a_b[u], zero_b)).astype(BF16) for u in units}
        inv = {u: inv[u] - _dot(de[u], inv_b[u]) for u in units}
    sol = {u: _dot(inv[u].astype(BF16), rhs_b[u]) for u in units}
    s = [state_ref[p] for p in pairs]
    o = {}
    for ch in range(n_ch):
        s_b = [s[p].astype(BF16) for p in pairs]
        v_cb = [(sol[ch, p][:, :GDN_DV] - _dot(blockdiag(sol[ch, p][:, GDN_DV:]), s_b[p])).astype(BF16)
                for p in pairs]
        for p in pairs:
            o[ch, p] = _dot(q2[ch, p], s_b[p]) + _dot(qk_b[ch, p], v_cb[p])
        for p in pairs:
            sdec = jnp.where(srow < GDN_DK, jnp.broadcast_to(e_gl[ch][0:1, p:p + 1], (2 * GDN_DK, LANE)),
                             jnp.broadcast_to(e_gl[ch][R - 1:R, p:p + 1], (2 * GDN_DK, LANE)))
            s[p] = s[p] * sdec + _dot_tn(k2[ch, p], v_cb[p])
    for p in pairs:
        state_ref[p] = s[p]
    for ch, p in units:
        z = stack(z_ref, 0, p, ch)
        on = o[ch, p] * lax.rsqrt(jnp.mean(o[ch, p] * o[ch, p], axis=-1, keepdims=True) + EPS) * ng_ref[...]
        res = (on * (z * _sigmoid(z))).astype(o_ref.dtype)
        rows = slice(ch * L, (ch + 1) * L)
        o_ref[rows, 2 * p * LANE:(2 * p + 1) * LANE] = res[:L]
        o_ref[rows, (2 * p + 1) * LANE:(2 * p + 2) * LANE] = res[L:]


def _gdn(conv, plain, small, dtb, alog, ng):
    t = conv.shape[0]
    L = GDN_CHUNKS_PER_STEP * GDN_CHUNK
    dz = GDN_HEADS * GDN_DV
    sm = lambda j: pl.BlockSpec((L, LANE), lambda c: (c, j))
    return pl.pallas_call(
        _gdn_kernel,
        grid=(t // L,),
        in_specs=[pl.BlockSpec((L, GDN_QKV), lambda c: (c, 0)),
                  pl.BlockSpec((L, dz), lambda c: (c, SSD_D_INNER // dz)),
                  sm(1), sm(2), sm(3), sm(4),
                  pl.BlockSpec((2, LANE), lambda c: (0, 0)),
                  pl.BlockSpec((2, LANE), lambda c: (0, 0)),
                  pl.BlockSpec((1, GDN_DV), lambda c: (0, 0))],
        out_specs=pl.BlockSpec((L, dz), lambda c: (c, 0)),
        out_shape=jax.ShapeDtypeStruct((t, dz), BF16),
        scratch_shapes=[pltpu.VMEM((GDN_HEADS // 2, 2 * GDN_DK, GDN_DV), F32)],
        compiler_params=_params(("arbitrary",), 40),
        name="gdn",
    )(conv, plain, small, small, small, small, dtb, alog, ng)


def _attn_kernel(q_ref, qnext_ref, k_ref, v_ref, qg_ref, kg_ref, bias_ref, o_ref,
                 kaug_ref, vaug_ref, kmean_ref, s_ref, qaug_ref):
    qt = pl.program_id(1)
    B = MOBA_BLOCK
    t = k_ref.shape[0]
    nb = t // B

    def query_operand(q, own):
        qn = q * lax.rsqrt(jnp.mean(q * q, axis=-1, keepdims=True) + EPS) * qg_ref[...]
        gate = lax.dot_general(kmean_ref[...], qn, (((1,), (1,)), ((), ())), preferred_element_type=F32,
                               precision=lax.Precision.HIGHEST)
        blk = lax.broadcasted_iota(jnp.int32, gate.shape, 0)
        past = blk < own
        gate = jnp.where(past, gate, NEG_INF)
        chosen = blk == own
        blk_f = blk.astype(F32)
        for _ in range(MOBA_TOPK):
            best = jnp.max(gate, axis=0, keepdims=True)
            first = jnp.min(jnp.where(gate == best, blk_f, float(nb)), axis=0, keepdims=True)
            hit = blk_f == first
            chosen = chosen | (hit & past)
            gate = jnp.where(hit, -jnp.inf, gate)
        selb_t = jnp.where(chosen, 0.0, NEG_INF)
        selb_t = jnp.concatenate([selb_t, jnp.zeros((LANE - nb, B), F32)], axis=0)
        q_t = (qn * (ATTN_HEAD_DIM ** -0.5 * LOG2E)).T
        return jnp.concatenate([q_t.astype(BF16), selb_t.astype(BF16)], axis=0)

    @pl.when(qt == 0)
    def _():
        lane_b = lax.broadcasted_iota(jnp.int32, (B, LANE), 1)
        vaug_ref[ATTN_HEAD_DIM:ATTN_VT_ROWS, :] = jnp.ones((ATTN_VT_ROWS - ATTN_HEAD_DIM, t), BF16)
        for b in range(nb):
            rows = slice(b * B, (b + 1) * B)
            k = k_ref[rows, :]
            kn = k * lax.rsqrt(jnp.mean(k * k, axis=-1, keepdims=True) + EPS) * kg_ref[...]
            kaug_ref[rows, 0:LANE] = kn.astype(BF16)
            kaug_ref[rows, LANE:2 * LANE] = jnp.where(lane_b == b, 1.0, 0.0).astype(BF16)
            vaug_ref[0:ATTN_HEAD_DIM, rows] = v_ref[rows, :].T.astype(BF16)
            kmean_ref[b:b + 1, :] = jnp.mean(kn, axis=0, keepdims=True)
        qaug_ref[0] = query_operand(q_ref[...], 0)

    own = qt
    slot = lax.rem(qt, 2)

    def attend(n_blocks):
        q_aug_t = qaug_ref[slot]
        m = jnp.full((1, B), NEG_INF, F32)
        for j0 in range(0, n_blocks, ATTN_QK_BLOCKS):
            n_qk = min(ATTN_QK_BLOCKS, n_blocks - j0)
            s_all = _dot(kaug_ref[j0 * B:(j0 + n_qk) * B, :], q_aug_t)
            for u in range(n_qk):
                j = j0 + u
                tile = jnp.where(j > own, N_BIAS_TILES - 1, jnp.minimum(own - j, N_BIAS_TILES - 2))
                s = s_all[u * B:(u + 1) * B] + bias_ref[tile]
                s_ref[j] = s
                m = jnp.maximum(m, jnp.max(s, axis=0, keepdims=True))
        qaug_ref[1 - slot] = query_operand(qnext_ref[...], qt + 1)
        acc = jnp.zeros((ATTN_VT_ROWS, B), F32)
        for j in range(n_blocks):
            p = jnp.exp2(s_ref[j] - m)
            acc = acc + _dot(vaug_ref[:, j * B:(j + 1) * B], p.astype(BF16))
        out_t = acc[:ATTN_HEAD_DIM] / acc[ATTN_HEAD_DIM:ATTN_HEAD_DIM + 1]
        o_ref[...] = out_t.T.astype(o_ref.dtype)

    n_branches = -(-nb // ATTN_UNROLL)
    for i in range(n_branches):
        @pl.when(own // ATTN_UNROLL == i)
        def _(i=i):
            attend(min((i + 1) * ATTN_UNROLL, nb))


def _attn(plain, qg, kg, bias_tiles):
    t = plain.shape[0]
    B = MOBA_BLOCK
    hd = ATTN_HEAD_DIM
    base = (SSD_D_INNER + GDN_HEADS * GDN_DV) // hd
    return pl.pallas_call(
        _attn_kernel,
        grid=(ATTN_HEADS, t // B),
        in_specs=[pl.BlockSpec((B, hd), lambda h, i: (i, base + h)),
                  pl.BlockSpec((B, hd), lambda h, i: (jnp.minimum(i + 1, t // B - 1), base + h)),
                  pl.BlockSpec((t, hd), lambda h, i: (0, base + ATTN_HEADS + h)),
                  pl.BlockSpec((t, hd), lambda h, i: (0, base + 2 * ATTN_HEADS + h)),
                  pl.BlockSpec((1, hd), lambda h, i: (0, 0)),
                  pl.BlockSpec((1, hd), lambda h, i: (0, 0)),
                  pl.BlockSpec((None, N_BIAS_TILES, B, B), lambda h, i: (h, 0, 0, 0))],
        out_specs=pl.BlockSpec((B, hd), lambda h, i: (i, h)),
        out_shape=jax.ShapeDtypeStruct((t, ATTN_HEADS * hd), BF16),
        scratch_shapes=[pltpu.VMEM((t, 2 * LANE), BF16), pltpu.VMEM((ATTN_VT_ROWS, t), BF16),
                        pltpu.VMEM((t // B, hd), F32), pltpu.VMEM((t // B, B, B), F32),
                        pltpu.VMEM((2, 2 * LANE, B), BF16)],
        compiler_params=_params(("arbitrary", "arbitrary"), 56),
        name="moba",
    )(plain, plain, plain, plain, qg, kg, bias_tiles)


def _bias_tiles(rel_bias, t):
    B = MOBA_BLOCK
    dist = jnp.arange(t, dtype=jnp.int32)
    exact = REL_BUCKETS // 2
    nf = jnp.maximum(dist, 1).astype(F32)
    large = exact + (jnp.log(nf / exact) / math.log(REL_MAX_DIST / exact) * (REL_BUCKETS - exact)).astype(jnp.int32)
    bucket = jnp.where(dist < exact, dist, jnp.minimum(large, REL_BUCKETS - 1))
    by_dist = rel_bias.T[:, bucket] * LOG2E
    nh = by_dist.shape[0]
    i = jnp.arange(B)[:, None]
    j = jnp.arange(B)[None, :]
    tiles = []
    for d in range(N_BIAS_TILES - 2):
        idx = jnp.clip(d * B + jnp.arange(2 * B - 1) - (B - 1), 0, t - 1)
        row = jnp.pad(by_dist[:, idx], ((0, 0), (0, 1)))
        skew = jnp.broadcast_to(row[:, None, :], (nh, B, 2 * B)).reshape(nh, 2 * B * B)
        tile = skew[:, :B * (2 * B - 1)].reshape(nh, B, 2 * B - 1)[:, :, B - 1:]
        if d == 0:
            tile = jnp.where(j >= i, tile, NEG_INF)
        tiles.append(tile)
    far = jnp.broadcast_to((rel_bias.T[:, REL_BUCKETS - 1] * LOG2E)[:, None, None], tiles[0].shape)
    tiles.append(far)
    tiles.append(jnp.full_like(far, NEG_INF))
    return jnp.stack(tiles, axis=1)


def _merge_kernel(y1_ref, y2_ref, y3_ref, w1_ref, w2_ref, w3_ref, g1_ref, g2_ref, g3_ref, o_ref):
    acc = _sigmoid(g1_ref[...]) * _dot(y1_ref[...], w1_ref[...])
    acc = acc + _sigmoid(g2_ref[...]) * _dot(y2_ref[...], w2_ref[...])
    acc = acc + _sigmoid(g3_ref[...]) * _dot(y3_ref[...], w3_ref[...])
    o_ref[...] = acc.astype(o_ref.dtype)


def _merge(y_ssd, y_gdn, y_attn, w1, w2, w3, plain, l):
    t, d = y_ssd.shape
    tm, tn = 512, 512
    gbase =(SSD_D_INNER + GDN_HEADS * GDN_DV + ATTN_QKV) // tn
    yspec = pl.BlockSpec((tm, d), lambda i, j: (i, 0))
    wspec = pl.BlockSpec((None, d, tn), lambda i, j: (l, 0, j))
    gspec = lambda b: pl.BlockSpec((tm, tn), lambda i, j: (i, gbase + b * (D_MODEL // tn) + j))
    return pl.pallas_call(
        _merge_kernel,
        grid=(t // tm, D_MODEL // tn),
        in_specs=[yspec, yspec, yspec, wspec, wspec, wspec, gspec(0), gspec(1), gspec(2)],
        out_specs=pl.BlockSpec((tm, tn), lambda i, j: (i, j)),
        out_shape=jax.ShapeDtypeStruct((t, D_MODEL), BF16),
        compiler_params=_params(("parallel", "parallel"), 48),
        name="merge",
    )(y_ssd, y_gdn, y_attn, w1, w2, w3, plain, plain, plain)


def _out_kernel(x_ref, w_ref, h_ref, g_ref, o_ref):
    o_ref[...] = h_ref[...] + g_ref[...] * _dot(x_ref[...], w_ref[...])


def _out_proj(x, w, h, g, l):
    t, k = x.shape
    n = w.shape[-1]
    tm, tn = 1024, 1024
    return pl.pallas_call(
        _out_kernel,
        grid=(n // tn, t // tm),
        in_specs=[pl.BlockSpec((tm, k), lambda j, i: (i, 0)),
                  pl.BlockSpec((None, k, tn), lambda j, i: (l, 0, j)),
                  pl.BlockSpec((tm, tn), lambda j, i: (i, j)),
                  pl.BlockSpec((1, tn), lambda j, i: (0, j))],
        out_specs=pl.BlockSpec((tm, tn), lambda j, i: (i, j)),
        out_shape=jax.ShapeDtypeStruct((t, n), F32),
        compiler_params=_params(("parallel", "parallel"), 48),
        name="out_proj",
    )(x, w, h, g)


def _pad_lanes(w, n=LANE):
    return jnp.pad(w, [(0, 0)] * (w.ndim - 1) + [(0, n - w.shape[-1])])


def _split_in(w_in):
    cuts = []
    lo = 0
    for s in IN_SPLITS:
        cuts.append(w_in[..., lo:lo + s])
        lo += s
    return cuts


def _in_segments():
    names = ("ssd_z", "ssd_xbc", "ssd_dt", "gdn_qkv", "gdn_z", "gdn_a", "gdn_b", "attn_qkv", "gates")
    out, lo = {}, 0
    for name, size in zip(names, IN_SPLITS):
        out[name] = (lo, size)
        lo += size
    return out


def kernel(x, c, w_mod, b_mod, norm_ffn1, ffn1_w_gate, ffn1_w_up, ffn1_w_down, norm_mix, w_in, ssd_conv_w, ssd_conv_b, ssd_dt_bias, ssd_a_log, ssd_d, ssd_norm, w_o_ssd, gdn_conv_w, gdn_dt_bias, gdn_a_log, gdn_norm, w_o_gdn, attn_q_norm, attn_k_norm, rel_bias, w_o_attn, w_out, norm_ffn2, ffn2_w_gate, ffn2_w_up, ffn2_w_down):
    bsz, t, d = x.shape
    assert bsz == 1 and d == D_MODEL and t % 1024 == 0 and t // MOBA_BLOCK <= LANE
    depth = w_mod.shape[0]

    w_in_t = jnp.swapaxes(w_in, 1, 2)
    seg = _in_segments()
    conv_segments = (seg["gdn_qkv"], seg["ssd_xbc"])
    plain_segments = (seg["ssd_z"], seg["gdn_z"], seg["attn_qkv"], seg["gates"])
    _, _, ssd_dt, _, _, gdn_a, gdn_b, _, _ = _split_in(w_in)
    w_small =jnp.concatenate([_pad_lanes(ssd_dt), _pad_lanes(gdn_a[..., 0::2]), _pad_lanes(gdn_a[..., 1::2]),
                               _pad_lanes(gdn_b[..., 0::2]), _pad_lanes(gdn_b[..., 1::2])], axis=-1).astype(BF16)
    conv_w = jnp.concatenate([gdn_conv_w, ssd_conv_w], axis=-1)
    conv_b = jnp.concatenate([jnp.zeros((depth, GDN_QKV), F32), ssd_conv_b], axis=-1).reshape(depth, 1, CONV_N)
    ssd_dtb = _pad_lanes(ssd_dt_bias)
    ssd_alog = _pad_lanes(ssd_a_log)
    ssd_dskip = jnp.repeat(ssd_d, SSD_HEAD_DIM, axis=-1)
    gdn_dtb = jnp.stack([_pad_lanes(gdn_dt_bias[:, 0::2]), _pad_lanes(gdn_dt_bias[:, 1::2])], axis=1)
    gdn_alog = jnp.stack([_pad_lanes(gdn_a_log[:, 0::2]), _pad_lanes(gdn_a_log[:, 1::2])], axis=1)
    f1g, f1u, f1d = ffn1_w_gate, ffn1_w_up, ffn1_w_down
    f2g, f2u, f2d = ffn2_w_gate, ffn2_w_up, ffn2_w_down
    wo_ssd, wo_gdn, wo_attn, wo = (w_o_ssd.astype(BF16), w_o_gdn.astype(BF16), w_o_attn.astype(BF16),
                                   w_out.astype(BF16))
    bias_tiles = _bias_tiles(rel_bias, t)

    mod = _mod_all(c, w_mod, b_mod).reshape(depth, N_MOD, d)
    h = x.reshape(t, d)
    for l in range(depth):
        sh1, sc1, g1, sh2, sc2, g2, sh3, sc3, g3 = [mod[l, i][None, :] for i in range(N_MOD)]
        h = _ffn(h, norm_ffn1[l][None, :], sh1, sc1, g1, f1g, f1u, f1d, l)
        u = _modulate(h, norm_mix[l][None, :], sh2, sc2)
        conv = _matmul_conv(u, w_in_t, conv_w, conv_b, l, conv_segments)
        plain = _matmul_t(u, w_in_t, l, plain_segments)
        small = _matmul(u, w_small, l, tn=SMALL_N)
        y_ssd = _ssd(conv, plain, small, ssd_dtb[l][None, :], ssd_alog[l][None, :], ssd_dskip[l][None, :],
                     ssd_norm[l][None, :])
        y_gdn = _gdn(conv, plain, small, gdn_dtb[l], gdn_alog[l], gdn_norm[l][None, :])
        y_attn = _attn(plain, attn_q_norm[l][None, :], attn_k_norm[l][None, :], bias_tiles)
        merged = _merge(y_ssd, y_gdn, y_attn, wo_ssd, wo_gdn, wo_attn, plain, l)
        h = _out_proj(merged, wo, h, g2, l)
        h = _ffn(h, norm_ffn2[l][None, :], sh3, sc3, g3, f2g, f2u, f2d, l)
    return h.reshape(bsz, t, d)
```

```python
import functools
import math

import jax
import jax.numpy as jnp
from jax import lax
from jax.experimental import pallas as pl
from jax.experimental.pallas import tpu as pltpu

F32 = jnp.float32
BF16 = jnp.bfloat16

D_MODEL = 2048
DEPTH = 4
D_FF = 5632
CONV_K = 4
SSD_HEADS = 32
SSD_HEAD_DIM = 64
SSD_D_INNER = SSD_HEADS * SSD_HEAD_DIM
SSD_GROUPS = 4
SSD_STATE = 128
SSD_CHUNK = 128
GDN_HEADS = 16
GDN_DK = 128
GDN_DV = 128
GDN_CHUNK = 64
ATTN_HEADS = 16
ATTN_HEAD_DIM = 128
MOBA_BLOCK = 256
MOBA_TOPK = 3
REL_BUCKETS = 32
REL_MAX_DIST = 4096
N_BRANCHES = 3
N_MOD = 9
EPS = 1e-6
NEG_INF = -1e30

SSD_XBC = SSD_D_INNER + 2 * SSD_GROUPS * SSD_STATE
GDN_QKV = GDN_HEADS * (2 * GDN_DK + GDN_DV)
ATTN_QKV = 3 * ATTN_HEADS * ATTN_HEAD_DIM
IN_SPLITS = (SSD_D_INNER, SSD_XBC, SSD_HEADS, GDN_QKV, GDN_HEADS * GDN_DV, GDN_HEADS, GDN_HEADS, ATTN_QKV,
             N_BRANCHES * D_MODEL)

LANE = 128
V7X_VMEM_BYTES = 64 * 1024 * 1024
N_BIAS_TILES = 15
ATTN_UNROLL = 2
ATTN_QK_BLOCKS = 2
GDN_CHUNKS_PER_STEP = 2
ATTN_VT_ROWS = ATTN_HEAD_DIM + 16
LOG2E = 1.4426950408889634
CONV_COL_CHUNK = 256
CONV_N = GDN_QKV + SSD_XBC
PLAIN_N = SSD_D_INNER + GDN_HEADS * GDN_DV + ATTN_QKV + N_BRANCHES * D_MODEL
SMALL_N = 5 * LANE


def _params(semantics, vmem_mb):
    return pltpu.CompilerParams(dimension_semantics=semantics, vmem_limit_bytes=vmem_mb * 1024 * 1024)


def _sigmoid(x):
    return 1.0 / (1.0 + jnp.exp(-x))


def _softplus(x):
    return jnp.maximum(x, 0.0) + jnp.log1p(jnp.exp(-jnp.abs(x)))


def _dot(a, b):
    return jnp.dot(a, b, preferred_element_type=F32)


def _dot_nt(a, b):
    return lax.dot_general(a, b, (((1,), (1,)), ((), ())), preferred_element_type=F32)


def _dot_tn(a, b):
    return lax.dot_general(a, b, (((0,), (0,)), ((), ())), preferred_element_type=F32)


def _dot_hi(a, b):
    return jnp.dot(a, b, preferred_element_type=F32, precision=lax.Precision.HIGHEST)


def _mod_kernel(cb_ref, w_ref, b_ref, o_ref):
    cb = cb_ref[...]
    tn = w_ref.shape[-1]
    for j in range(tn // LANE):
        sl = slice(j * LANE, (j + 1) * LANE)
        o_ref[:, sl] = jnp.sum(w_ref[:, sl] * cb, axis=0, keepdims=True) + b_ref[:, sl]


def _mod_all(c, w_mod, b_mod):
    depth, d, n = w_mod.shape
    tn = 1024
    cb = jnp.broadcast_to(c.reshape(d, 1), (d, LANE))
    return pl.pallas_call(
        _mod_kernel,
        grid=(depth, n // tn),
        in_specs=[pl.BlockSpec((d, LANE), lambda l, j: (0, 0)),
                  pl.BlockSpec((None, d, tn), lambda l, j: (l, 0, j)),
                  pl.BlockSpec((None, 1, tn), lambda l, j: (l, 0, j))],
        out_specs=pl.BlockSpec((None, 1, tn), lambda l, j: (l, 0, j)),
        out_shape=jax.ShapeDtypeStruct((depth, 1, n), F32),
        compiler_params=_params(("parallel", "parallel"), 40),
        name="adaln_mod",
    )(cb, w_mod, b_mod.reshape(depth, 1, n))


def _modulated(x, gn, sh, sc):
    xn = x * lax.rsqrt(jnp.mean(x * x, axis=-1, keepdims=True) + EPS)
    return (xn * gn) * (1.0 + sc) + sh


def _modulate_kernel(h_ref, gn_ref, sh_ref, sc_ref, o_ref):
    o_ref[...] = _modulated(h_ref[...], gn_ref[...], sh_ref[...], sc_ref[...]).astype(o_ref.dtype)


def _modulate(h, gn, sh, sc):
    t, d = h.shape
    tm = 512
    vec = pl.BlockSpec((1, d), lambda i: (0, 0))
    return pl.pallas_call(
        _modulate_kernel,
        grid=(t // tm,),
        in_specs=[pl.BlockSpec((tm, d), lambda i: (i, 0)), vec, vec, vec],
        out_specs=pl.BlockSpec((tm, d), lambda i: (i, 0)),
        out_shape=jax.ShapeDtypeStruct((t, d), BF16),
        compiler_params=_params(("parallel",), 32),
        name="modulate",
    )(h, gn, sh, sc)


def _ffn_kernel(h_ref, gn_ref, sh_ref, sc_ref, g_ref, wg_ref, wu_ref, wd_ref, o_ref, xm_ref):
    f = pl.program_id(1)

    @pl.when(f == 0)
    def _():
        xm_ref[...] = _modulated(h_ref[...], gn_ref[...], sh_ref[...], sc_ref[...]).astype(BF16)
        o_ref[...] = jnp.zeros_like(o_ref)

    xm = xm_ref[...]
    a = _dot(xm, wg_ref[...].astype(BF16))
    b = _dot(xm, wu_ref[...].astype(BF16))
    hm = ((a * _sigmoid(a)) * b).astype(BF16)
    o_ref[...] += _dot(hm, wd_ref[...].astype(BF16))

    @pl.when(f == pl.num_programs(1) - 1)
    def _():
        o_ref[...] = h_ref[...] + (0.5 * g_ref[...]) * o_ref[...]


def _ffn(h, gn, sh, sc, g, wg, wu, wd, l):
    t, d = h.shape
    ff = wg.shape[-1]
    tm, tf = 1024, 256
    vec = pl.BlockSpec((1, d), lambda i, f: (0, 0))
    return pl.pallas_call(
        _ffn_kernel,
        grid=(t // tm, ff // tf),
        in_specs=[pl.BlockSpec((tm, d), lambda i, f: (i, 0)), vec, vec, vec, vec,
                  pl.BlockSpec((None, d, tf), lambda i, f: (l, 0, f)),
                  pl.BlockSpec((None, d, tf), lambda i, f: (l, 0, f)),
                  pl.BlockSpec((None, tf, d), lambda i, f: (l, f, 0))],
        out_specs=pl.BlockSpec((tm, d), lambda i, f: (i, 0)),
        out_shape=jax.ShapeDtypeStruct((t, d), F32),
        scratch_shapes=[pltpu.VMEM((tm, d), BF16)],
        compiler_params=_params(("parallel", "arbitrary"), 58),
        name="ffn",
    )(h, gn, sh, sc, g, wg, wu, wd)


def _mm_kernel(x_ref, w_ref, o_ref):
    o_ref[...] = _dot(x_ref[...], w_ref[...]).astype(o_ref.dtype)


def _matmul(x, w, l, tn, tm=1024, out_dtype=F32):
    t, k = x.shape
    n = w.shape[-1]
    return pl.pallas_call(
        _mm_kernel,
        grid=(n // tn, t // tm),
        in_specs=[pl.BlockSpec((tm, k), lambda j, i: (i, 0)),
                  pl.BlockSpec((None, k, tn), lambda j, i: (l, 0, j))],
        out_specs=pl.BlockSpec((tm, tn), lambda j, i: (i, j)),
        out_shape=jax.ShapeDtypeStruct((t, n), out_dtype),
        compiler_params=_params(("parallel", "parallel"), 48),
        name="proj",
    )(x, w)


def _segment_row(j, tn, segments):
    col = j * tn
    row = col
    out_lo = 0
    for src, size in segments:
        assert size % tn == 0 and src % 8 == 0
        row = jnp.where(col >= out_lo, src + (col - out_lo), row)
        out_lo += size
    return pl.multiple_of(row, 8)


def _wt_spec(l, tn, k, segments):
    return pl.BlockSpec((pl.Element(1), pl.Element(tn), pl.Element(k)),
                        lambda j, i: (l, _segment_row(j, tn, segments), 0))


def _mm_t_kernel(x_ref, wt_ref, o_ref, wb_ref):
    @pl.when(pl.program_id(1) == 0)
    def _():
        wb_ref[...] = wt_ref[0].astype(BF16)

    o_ref[...] = _dot_nt(x_ref[...], wb_ref[...]).astype(o_ref.dtype)


def _matmul_t(x, w_t, l, segments, tn=1024, tm=1024):
    t, k = x.shape
    n = sum(size for _, size in segments)
    return pl.pallas_call(
        _mm_t_kernel,
        grid=(n // tn, t // tm),
        in_specs=[pl.BlockSpec((tm, k), lambda j, i: (i, 0)), _wt_spec(l, tn, k, segments)],
        out_specs=pl.BlockSpec((tm, tn), lambda j, i: (i, j)),
        out_shape=jax.ShapeDtypeStruct((t, n), F32),
        scratch_shapes=[pltpu.VMEM((tn, k), BF16)],
        compiler_params=_params(("parallel", "arbitrary"), 48),
        name="proj",
    )(x, w_t)


def _mm_conv_kernel(x_ref, wt_ref, cw_ref, cb_ref, o_ref, win_ref, wb_ref):
    i = pl.program_id(1)

    @pl.when(i == 0)
    def _():
        win_ref[:, 0:8, :] = jnp.zeros((win_ref.shape[0], 8, win_ref.shape[2]), F32)
        wb_ref[...] = wt_ref[0].astype(BF16)

    tm, tn = o_ref.shape
    cc = CONV_COL_CHUNK
    n_chunks = tn // cc

    def project(c):
        hm = tm // 2
        for r in range(2):
            win_ref[c, 8 + r * hm:8 + (r + 1) * hm, :] = _dot_nt(x_ref[r * hm:(r + 1) * hm, :],
                                                                wb_ref[c * cc:(c + 1) * cc, :])

    def conv(c):
        cs = slice(c * cc, (c + 1) * cc)
        cw = cw_ref[:, cs]
        y = cb_ref[:, cs] + cw[CONV_K - 1:CONV_K, :] * win_ref[c, 8:8 + tm, :]
        for s in range(1, CONV_K):
            y = y + cw[CONV_K - 1 - s:CONV_K - s, :] * win_ref[c, 8 - s:8 - s + tm, :]
        o_ref[:, cs] = y * _sigmoid(y)
        win_ref[c, 0:8, :] = win_ref[c, tm:tm + 8, :]

    project(0)
    for c in range(1, n_chunks):
        project(c)
        conv(c - 1)
    conv(n_chunks - 1)


def _matmul_conv(x, w_t, cw, cb, l, segments, tn=1024, tm=1024):
    t, k = x.shape
    n = sum(size for _, size in segments)
    return pl.pallas_call(
        _mm_conv_kernel,
        grid=(n // tn, t // tm),
        in_specs=[pl.BlockSpec((tm, k), lambda j, i: (i, 0)),
                  _wt_spec(l, tn, k, segments),
                  pl.BlockSpec((None, CONV_K, tn), lambda j, i: (l, 0, j)),
                  pl.BlockSpec((None, 1, tn), lambda j, i: (l, 0, j))],
        out_specs=pl.BlockSpec((tm, tn), lambda j, i: (i, j)),
        out_shape=jax.ShapeDtypeStruct((t, n), F32),
        scratch_shapes=[pltpu.VMEM((tn // CONV_COL_CHUNK, tm + 8, CONV_COL_CHUNK), F32),
                        pltpu.VMEM((tn, k), BF16)],
        compiler_params=_params(("parallel", "arbitrary"), 56),
        name="proj_conv",
    )(x, w_t, cw, cb)


def _ssd_kernel(xbc_ref, z_ref, dt_ref, dtb_ref, alog_ref, dskip_ref, ng_ref, o_ref, state_ref):
    c = pl.program_id(0)

    @pl.when(c == 0)
    def _():
        state_ref[...] = jnp.zeros_like(state_ref)

    L = SSD_CHUNK
    P = SSD_HEAD_DIM
    assert L == LANE and 2 * P == LANE
    gw = SSD_D_INNER // SSD_GROUPS
    hpg = SSD_HEADS // SSD_GROUPS
    dt = _softplus(dt_ref[...] + dtb_ref[...])
    a = -jnp.exp(alog_ref[...])
    da = dt * a
    r_i = lax.broadcasted_iota(jnp.int32, (L, L), 0)
    c_i = lax.broadcasted_iota(jnp.int32, (L, L), 1)
    tri = r_i >= c_i
    acs = _dot_hi(tri.astype(F32), da)
    acs_t = acs.T
    last = acs[L - 1:L, :]
    dt_dec = dt * jnp.exp(last - acs)
    dt_t = dt.T
    e_last = jnp.exp(last)
    lane = lax.broadcasted_iota(jnp.int32, (L, LANE), 1)
    lo_half = lane < P

    def pair_bcast(v, h):
        rows = v.shape[0]
        return jnp.where(lo_half[:rows], jnp.broadcast_to(v[:, h:h + 1], (rows, LANE)),
                         jnp.broadcast_to(v[:, h + 1:h + 2], (rows, LANE)))

    for g in range(SSD_GROUPS):
        bg = xbc_ref[:, SSD_D_INNER + g * SSD_STATE:SSD_D_INNER + (g + 1) * SSD_STATE].astype(BF16)
        cg = xbc_ref[:, SSD_D_INNER + SSD_GROUPS * SSD_STATE + g * SSD_STATE:
                     SSD_D_INNER + SSD_GROUPS * SSD_STATE + (g + 1) * SSD_STATE].astype(BF16)
        cb = _dot_nt(cg, bg)
        s_in = state_ref[g]
        y_off = _dot(cg, s_in.astype(BF16))
        xdec_parts = []
        dec_parts = []
        y_parts = []
        for hp in range(hpg // 2):
            h = g * hpg + 2 * hp
            ch = slice(h * P, (h + 2) * P)
            xs = xbc_ref[:, ch]
            gmats = []
            acs_cols = []
            for hh in (h, h + 1):
                acs_col = jnp.broadcast_to(acs[:, hh:hh + 1], (L, L))
                acs_cols.append(acs_col)
                diff = acs_col - acs_t[hh:hh + 1, :]
                lm = jnp.exp(jnp.where(tri, diff, -jnp.inf))
                gmats.append((lm * cb * dt_t[hh:hh + 1, :]).astype(BF16))
            lhs = jnp.concatenate(gmats, axis=1)
            rhs = jnp.concatenate([jnp.where(lo_half, xs, 0.0), jnp.where(lo_half, 0.0, xs)], axis=0).astype(BF16)
            y = _dot(lhs, rhs)
            y = y + y_off[:, 2 * hp * P:(2 * hp + 2) * P] * jnp.exp(jnp.where(lo_half, acs_cols[0], acs_cols[1]))
            y = y + xs * dskip_ref[:, ch]
            zz = z_ref[:, ch]
            y_parts.append(y * (zz * _sigmoid(zz)))
            xdec_parts.append((xs * pair_bcast(dt_dec, h)).astype(BF16))
            dec_parts.append(pair_bcast(e_last, h))
        xdec = jnp.concatenate(xdec_parts, axis=1)
        dec = jnp.concatenate(dec_parts, axis=1)
        state_ref[g] = s_in * dec + _dot_tn(bg, xdec)
        ssq = sum(jnp.sum(y * y, axis=-1, keepdims=True) for y in y_parts)
        inv_rms = lax.rsqrt(ssq * (1.0 / gw) + EPS)
        for i, y in enumerate(y_parts):
            ch = slice(g * gw + i * LANE, g * gw + (i + 1) * LANE)
            o_ref[:, ch] = (y * inv_rms * ng_ref[:, ch]).astype(o_ref.dtype)


def _ssd(conv, plain, small, dtb, alog, dskip, ng):
    t = conv.shape[0]
    L = SSD_CHUNK
    xbc_blk = GDN_QKV // SSD_XBC
    vec = lambda n: pl.BlockSpec((1, n), lambda c: (0, 0))
    return pl.pallas_call(
        _ssd_kernel,
        grid=(t // L,),
        in_specs=[pl.BlockSpec((L, SSD_XBC), lambda c: (c, xbc_blk)),
                  pl.BlockSpec((L, SSD_D_INNER), lambda c: (c, 0)),
                  pl.BlockSpec((L, LANE), lambda c: (c, 0)),
                  vec(LANE), vec(LANE), vec(SSD_D_INNER), vec(SSD_D_INNER)],
        out_specs=pl.BlockSpec((L, SSD_D_INNER), lambda c: (c, 0)),
        out_shape=jax.ShapeDtypeStruct((t, SSD_D_INNER), BF16),
        scratch_shapes=[pltpu.VMEM((SSD_GROUPS, SSD_STATE, SSD_D_INNER // SSD_GROUPS), F32)],
        compiler_params=_params(("arbitrary",), 32),
        name="ssd",
    )(conv, plain, small, dtb, alog, dskip, ng)


def _gdn_kernel(qkv_ref, z_ref, ae_ref, ao_ref, be_ref, bo_ref, dtb_ref, alog_ref, ng_ref, o_ref, state_ref):
    c = pl.program_id(0)

    @pl.when(c == 0)
    def _():
        state_ref[...] = jnp.zeros_like(state_ref)

    L = GDN_CHUNK
    R = 2 * L
    n_pairs = GDN_HEADS // 2
    n_ch = qkv_ref.shape[0] // L
    row1 = lax.broadcasted_iota(jnp.int32, (R, LANE), 0)
    top = row1 < L
    dtb2 = jnp.where(top, dtb_ref[0:1, :], dtb_ref[1:2, :])
    alog2 = jnp.where(top, alog_ref[0:1, :], alog_ref[1:2, :])
    r_i = lax.broadcasted_iota(jnp.int32, (R, R), 0)
    c_i = lax.broadcasted_iota(jnp.int32, (R, R), 1)
    same = (r_i < L) == (c_i < L)
    incl = same & (r_i >= c_i)
    strict = same & (r_i > c_i)
    incl_f = incl.astype(F32)
    eye = (r_i == c_i).astype(F32)
    merge_masks = []
    for lb in range(int(math.log2(L))):
        r_blk = jnp.right_shift(r_i, lb)
        c_blk = jnp.right_shift(c_i, lb)
        merge_masks.append((r_blk == c_blk + 1) & (jnp.bitwise_and(r_blk, 1) == 1))
    row2 =lax.broadcasted_iota(jnp.int32, (R, 2 * LANE), 0)
    col2 = lax.broadcasted_iota(jnp.int32, (R, 2 * LANE), 1)
    diag2 = (row2 < L) == (col2 < LANE)
    srow = lax.broadcasted_iota(jnp.int32, (2 * GDN_DK, LANE), 0)

    def stack(ref, base, p, ch):
        lo = base + 2 * p * LANE
        rows = slice(ch * L, (ch + 1) * L)
        return jnp.concatenate([ref[rows, lo:lo + LANE], ref[rows, lo + LANE:lo + 2 * LANE]], axis=0)

    def blockdiag(x):
        return jnp.where(diag2, jnp.concatenate([x, x], axis=1), 0.0).astype(BF16)

    def col(v, p):
        return jnp.broadcast_to(v[:, p:p + 1], (R, LANE))

    pairs = range(n_pairs)
    units = [(ch, p) for ch in range(n_ch) for p in pairs]
    a_b, qk_b, rhs_b, q2, k2, e_gl = {}, {}, {}, {}, {}, {}
    for ch in range(n_ch):
        rows = slice(ch * L, (ch + 1) * L)
        a2 = jnp.concatenate([ae_ref[rows, :], ao_ref[rows, :]], axis=0)
        b2 = jnp.concatenate([be_ref[rows, :], bo_ref[rows, :]], axis=0)
        beta2 = _sigmoid(b2)
        g2 = -jnp.exp(alog2) * _softplus(a2 + dtb2)
        gc2 = _dot_hi(incl_f, g2)
        gc2_t = gc2.T
        gl2 = jnp.where(top, gc2[L - 1:L, :], gc2[R - 1:R, :])
        e_gc = jnp.exp(gc2)
        e_rem = jnp.exp(gl2 - gc2)
        e_gl[ch] = jnp.exp(gl2)
        for p in pairs:
            u = (ch, p)
            q = stack(qkv_ref, 0, p, ch)
            k = stack(qkv_ref, GDN_HEADS * GDN_DK, p, ch)
            v = stack(qkv_ref, 2 * GDN_HEADS * GDN_DK, p, ch)
            qn = q * lax.rsqrt(jnp.sum(q * q, axis=-1, keepdims=True) + EPS) * (GDN_DK ** -0.5)
            kn = k * lax.rsqrt(jnp.sum(k * k, axis=-1, keepdims=True) + EPS)
            beta = col(beta2, p)
            diff = gc2[:, p:p + 1] - gc2_t[p:p + 1, :]
            decay = jnp.exp(jnp.where(incl, diff, -jnp.inf))
            kb = kn * beta
            kn_b = kn.astype(BF16)
            a_b[u] = jnp.where(strict, _dot_nt(kb.astype(BF16), kn_b) * decay, 0.0).astype(BF16)
            qk_b[u] = jnp.where(incl, _dot_nt(qn.astype(BF16), kn_b) * decay, 0.0).astype(BF16)
            rhs_b[u] = jnp.concatenate([v * beta, kb * col(e_gc, p)], axis=1).astype(BF16)
            q2[u] = blockdiag(qn * col(e_gc, p))
            k2[u] = blockdiag(kn * col(e_rem, p))
    zero_b = jnp.zeros((R, R), BF16)
    inv = {u: eye - jnp.where(merge_masks[0], a_b[u], zero_b).astype(F32) for u in units}
    for mk in merge_masks[1:]:
        inv_b = {u: inv[u].astype(BF16) for u in units}
        de = {u: _dot(inv_b[u], jnp.where(mk, a_b[u], zero_b)).astype(BF16) for u in units}
        inv = {u: inv[u] - _dot(de[u], inv_b[u]) for u in units}
    sol = {u: _dot(inv[u].astype(BF16), rhs_b[u]) for u in units}
    s = [state_ref[p] for p in pairs]
    o = {}
    for ch in range(n_ch):
        s_b = [s[p].astype(BF16) for p in pairs]
        v_cb = [(sol[ch, p][:, :GDN_DV] - _dot(blockdiag(sol[ch, p][:, GDN_DV:]), s_b[p])).astype(BF16)
                for p in pairs]
        for p in pairs:
            o[ch, p] = _dot(q2[ch, p], s_b[p]) + _dot(qk_b[ch, p], v_cb[p])
        for p in pairs:
            sdec = jnp.where(srow < GDN_DK, jnp.broadcast_to(e_gl[ch][0:1, p:p + 1], (2 * GDN_DK, LANE)),
                             jnp.broadcast_to(e_gl[ch][R - 1:R, p:p + 1], (2 * GDN_DK, LANE)))
            s[p] = s[p] * sdec + _dot_tn(k2[ch, p], v_cb[p])
    for p in pairs:
        state_ref[p] = s[p]
    for ch, p in units:
        z = stack(z_ref, 0, p, ch)
        on = o[ch, p] * lax.rsqrt(jnp.mean(o[ch, p] * o[ch, p], axis=-1, keepdims=True) + EPS) * ng_ref[...]
        res = (on * (z * _sigmoid(z))).astype(o_ref.dtype)
        rows = slice(ch * L, (ch + 1) * L)
        o_ref[rows, 2 * p * LANE:(2 * p + 1) * LANE] = res[:L]
        o_ref[rows, (2 * p + 1) * LANE:(2 * p + 2) * LANE] = res[L:]


def _gdn(conv, plain, small, dtb, alog, ng):
    t = conv.shape[0]
    L = GDN_CHUNKS_PER_STEP * GDN_CHUNK
    dz = GDN_HEADS * GDN_DV
    sm = lambda j: pl.BlockSpec((L, LANE), lambda c: (c, j))
    return pl.pallas_call(
        _gdn_kernel,
        grid=(t // L,),
        in_specs=[pl.BlockSpec((L, GDN_QKV), lambda c: (c, 0)),
                  pl.BlockSpec((L, dz), lambda c: (c, SSD_D_INNER // dz)),
                  sm(1), sm(2), sm(3), sm(4),
                  pl.BlockSpec((2, LANE), lambda c: (0, 0)),
                  pl.BlockSpec((2, LANE), lambda c: (0, 0)),
                  pl.BlockSpec((1, GDN_DV), lambda c: (0, 0))],
        out_specs=pl.BlockSpec((L, dz), lambda c: (c, 0)),
        out_shape=jax.ShapeDtypeStruct((t, dz), BF16),
        scratch_shapes=[pltpu.VMEM((GDN_HEADS // 2, 2 * GDN_DK, GDN_DV), F32)],
        compiler_params=_params(("arbitrary",), 40),
        name="gdn",
    )(conv, plain, small, small, small, small, dtb, alog, ng)


def _attn_kernel(q_ref, qnext_ref, k_ref, v_ref, qg_ref, kg_ref, bias_ref, o_ref,
                 kaug_ref, vaug_ref, kmean_ref, s_ref, qaug_ref):
    qt = pl.program_id(1)
    B = MOBA_BLOCK
    t = k_ref.shape[0]
    nb = t // B

    def query_operand(q, own):
        qn = q * lax.rsqrt(jnp.mean(q * q, axis=-1, keepdims=True) + EPS) * qg_ref[...]
        gate = lax.dot_general(kmean_ref[...], qn, (((1,), (1,)), ((), ())), preferred_element_type=F32,
                               precision=lax.Precision.HIGHEST)
        blk = lax.broadcasted_iota(jnp.int32, gate.shape, 0)
        past = blk < own
        gate = jnp.where(past, gate, NEG_INF)
        chosen = blk == own
        blk_f = blk.astype(F32)
        for _ in range(MOBA_TOPK):
            best = jnp.max(gate, axis=0, keepdims=True)
            first = jnp.min(jnp.where(gate == best, blk_f, float(nb)), axis=0, keepdims=True)
            hit = blk_f == first
            chosen = chosen | (hit & past)
            gate = jnp.where(hit, -jnp.inf, gate)
        selb_t = jnp.where(chosen, 0.0, NEG_INF)
        selb_t = jnp.concatenate([selb_t, jnp.zeros((LANE - nb, B), F32)], axis=0)
        q_t = (qn * (ATTN_HEAD_DIM ** -0.5 * LOG2E)).T
        return jnp.concatenate([q_t.astype(BF16), selb_t.astype(BF16)], axis=0)

    @pl.when(qt == 0)
    def _():
        lane_b = lax.broadcasted_iota(jnp.int32, (B, LANE), 1)
        vaug_ref[ATTN_HEAD_DIM:ATTN_VT_ROWS, :] = jnp.ones((ATTN_VT_ROWS - ATTN_HEAD_DIM, t), BF16)
        for b in range(nb):
            rows = slice(b * B, (b + 1) * B)
            k = k_ref[rows, :]
            kn = k * lax.rsqrt(jnp.mean(k * k, axis=-1, keepdims=True) + EPS) * kg_ref[...]
            kaug_ref[rows, 0:LANE] = kn.astype(BF16)
            kaug_ref[rows, LANE:2 * LANE] = jnp.where(lane_b == b, 1.0, 0.0).astype(BF16)
            vaug_ref[0:ATTN_HEAD_DIM, rows] = v_ref[rows, :].T.astype(BF16)
            kmean_ref[b:b + 1, :] = jnp.mean(kn, axis=0, keepdims=True)
        qaug_ref[0] = query_operand(q_ref[...], 0)

    own = qt
    slot = lax.rem(qt, 2)

    def attend(n_blocks):
        q_aug_t = qaug_ref[slot]
        m = jnp.full((1, B), NEG_INF, F32)
        for j0 in range(0, n_blocks, ATTN_QK_BLOCKS):
            n_qk = min(ATTN_QK_BLOCKS, n_blocks - j0)
            s_all = _dot(kaug_ref[j0 * B:(j0 + n_qk) * B, :], q_aug_t)
            for u in range(n_qk):
                j = j0 + u
                tile = jnp.where(j > own, N_BIAS_TILES - 1, jnp.minimum(own - j, N_BIAS_TILES - 2))
                s = s_all[u * B:(u + 1) * B] + bias_ref[tile]
                s_ref[j] = s
                m = jnp.maximum(m, jnp.max(s, axis=0, keepdims=True))
        qaug_ref[1 - slot] = query_operand(qnext_ref[...], qt + 1)
        acc = jnp.zeros((ATTN_VT_ROWS, B), F32)
        for j in range(n_blocks):
            p = jnp.exp2(s_ref[j] - m)
            acc = acc + _dot(vaug_ref[:, j * B:(j + 1) * B], p.astype(BF16))
        out_t = acc[:ATTN_HEAD_DIM] / acc[ATTN_HEAD_DIM:ATTN_HEAD_DIM + 1]
        o_ref[...] = out_t.T.astype(o_ref.dtype)

    n_branches = -(-nb // ATTN_UNROLL)
    for i in range(n_branches):
        @pl.when(own // ATTN_UNROLL == i)
        def _(i=i):
            attend(min((i + 1) * ATTN_UNROLL, nb))


def _attn(plain, qg, kg, bias_tiles):
    t = plain.shape[0]
    B = MOBA_BLOCK
    hd = ATTN_HEAD_DIM
    base = (SSD_D_INNER + GDN_HEADS * GDN_DV) // hd
    return pl.pallas_call(
        _attn_kernel,
        grid=(ATTN_HEADS, t // B),
        in_specs=[pl.BlockSpec((B, hd), lambda h, i: (i, base + h)),
                  pl.BlockSpec((B, hd), lambda h, i: (jnp.minimum(i + 1, t // B - 1), base + h)),
                  pl.BlockSpec((t, hd), lambda h, i: (0, base + ATTN_HEADS + h)),
                  pl.BlockSpec((t, hd), lambda h, i: (0, base + 2 * ATTN_HEADS + h)),
                  pl.BlockSpec((1, hd), lambda h, i: (0, 0)),
                  pl.BlockSpec((1, hd), lambda h, i: (0, 0)),
                  pl.BlockSpec((None, N_BIAS_TILES, B, B), lambda h, i: (h, 0, 0, 0))],
        out_specs=pl.BlockSpec((B, hd), lambda h, i: (i, h)),
        out_shape=jax.ShapeDtypeStruct((t, ATTN_HEADS * hd), BF16),
        scratch_shapes=[pltpu.VMEM((t, 2 * LANE), BF16), pltpu.VMEM((ATTN_VT_ROWS, t), BF16),
                        pltpu.VMEM((t // B, hd), F32), pltpu.VMEM((t // B, B, B), F32),
                        pltpu.VMEM((2, 2 * LANE, B), BF16)],
        compiler_params=_params(("arbitrary", "arbitrary"), 56),
        name="moba",
    )(plain, plain, plain, plain, qg, kg, bias_tiles)


def _bias_tiles(rel_bias, t):
    B = MOBA_BLOCK
    dist = jnp.arange(t, dtype=jnp.int32)
    exact = REL_BUCKETS // 2
    nf = jnp.maximum(dist, 1).astype(F32)
    large = exact + (jnp.log(nf / exact) / math.log(REL_MAX_DIST / exact) * (REL_BUCKETS - exact)).astype(jnp.int32)
    bucket = jnp.where(dist < exact, dist, jnp.minimum(large, REL_BUCKETS - 1))
    by_dist = rel_bias.T[:, bucket] * LOG2E
    nh = by_dist.shape[0]
    i = jnp.arange(B)[:, None]
    j = jnp.arange(B)[None, :]
    tiles = []
    for d in range(N_BIAS_TILES - 2):
        idx = jnp.clip(d * B + jnp.arange(2 * B - 1) - (B - 1), 0, t - 1)
        row = jnp.pad(by_dist[:, idx], ((0, 0), (0, 1)))
        skew = jnp.broadcast_to(row[:, None, :], (nh, B, 2 * B)).reshape(nh, 2 * B * B)
        tile = skew[:, :B * (2 * B - 1)].reshape(nh, B, 2 * B - 1)[:, :, B - 1:]
        if d == 0:
            tile = jnp.where(j >= i, tile, NEG_INF)
        tiles.append(tile)
    far = jnp.broadcast_to((rel_bias.T[:, REL_BUCKETS - 1] * LOG2E)[:, None, None], tiles[0].shape)
    tiles.append(far)
    tiles.append(jnp.full_like(far, NEG_INF))
    return jnp.stack(tiles, axis=1)


def _merge_kernel(y1_ref, y2_ref, y3_ref, w1_ref, w2_ref, w3_ref, g1_ref, g2_ref, g3_ref, o_ref):
    acc = _sigmoid(g1_ref[...]) * _dot(y1_ref[...], w1_ref[...])
    acc = acc + _sigmoid(g2_ref[...]) * _dot(y2_ref[...], w2_ref[...])
    acc = acc + _sigmoid(g3_ref[...]) * _dot(y3_ref[...], w3_ref[...])
    o_ref[...] = acc.astype(o_ref.dtype)


def _merge(y_ssd, y_gdn, y_attn, w1, w2, w3, plain, l):
    t, d = y_ssd.shape
    tm, tn = 512, 512
    gbase =(SSD_D_INNER + GDN_HEADS * GDN_DV + ATTN_QKV) // tn
    yspec = pl.BlockSpec((tm, d), lambda i, j: (i, 0))
    wspec = pl.BlockSpec((None, d, tn), lambda i, j: (l, 0, j))
    gspec = lambda b: pl.BlockSpec((tm, tn), lambda i, j: (i, gbase + b * (D_MODEL // tn) + j))
    return pl.pallas_call(
        _merge_kernel,
        grid=(t // tm, D_MODEL // tn),
        in_specs=[yspec, yspec, yspec, wspec, wspec, wspec, gspec(0), gspec(1), gspec(2)],
        out_specs=pl.BlockSpec((tm, tn), lambda i, j: (i, j)),
        out_shape=jax.ShapeDtypeStruct((t, D_MODEL), BF16),
        compiler_params=_params(("parallel", "parallel"), 48),
        name="merge",
    )(y_ssd, y_gdn, y_attn, w1, w2, w3, plain, plain, plain)


def _out_kernel(x_ref, w_ref, h_ref, g_ref, o_ref):
    o_ref[...] = h_ref[...] + g_ref[...] * _dot(x_ref[...], w_ref[...])


def _out_proj(x, w, h, g, l):
    t, k = x.shape
    n = w.shape[-1]
    tm, tn = 1024, 1024
    return pl.pallas_call(
        _out_kernel,
        grid=(n // tn, t // tm),
        in_specs=[pl.BlockSpec((tm, k), lambda j, i: (i, 0)),
                  pl.BlockSpec((None, k, tn), lambda j, i: (l, 0, j)),
                  pl.BlockSpec((tm, tn), lambda j, i: (i, j)),
                  pl.BlockSpec((1, tn), lambda j, i: (0, j))],
        out_specs=pl.BlockSpec((tm, tn), lambda j, i: (i, j)),
        out_shape=jax.ShapeDtypeStruct((t, n), F32),
        compiler_params=_params(("parallel", "parallel"), 48),
        name="out_proj",
    )(x, w, h, g)


def _pad_lanes(w, n=LANE):
    return jnp.pad(w, [(0, 0)] * (w.ndim - 1) + [(0, n - w.shape[-1])])


def _split_in(w_in):
    cuts = []
    lo = 0
    for s in IN_SPLITS:
        cuts.append(w_in[..., lo:lo + s])
        lo += s
    return cuts


def _in_segments():
    names = ("ssd_z", "ssd_xbc", "ssd_dt", "gdn_qkv", "gdn_z", "gdn_a", "gdn_b", "attn_qkv", "gates")
    out, lo = {}, 0
    for name, size in zip(names, IN_SPLITS):
        out[name] = (lo, size)
        lo += size
    return out


def kernel(x, c, w_mod, b_mod, norm_ffn1, ffn1_w_gate, ffn1_w_up, ffn1_w_down, norm_mix, w_in, ssd_conv_w, ssd_conv_b, ssd_dt_bias, ssd_a_log, ssd_d, ssd_norm, w_o_ssd, gdn_conv_w, gdn_dt_bias, gdn_a_log, gdn_norm, w_o_gdn, attn_q_norm, attn_k_norm, rel_bias, w_o_attn, w_out, norm_ffn2, ffn2_w_gate, ffn2_w_up, ffn2_w_down):
    bsz, t, d = x.shape
    assert bsz == 1 and d == D_MODEL and t % 1024 == 0 and t // MOBA_BLOCK <= LANE
    depth = w_mod.shape[0]

    w_in_t = jnp.swapaxes(w_in, 1, 2)
    seg = _in_segments()
    conv_segments = (seg["gdn_qkv"], seg["ssd_xbc"])
    plain_segments = (seg["ssd_z"], seg["gdn_z"], seg["attn_qkv"], seg["gates"])
    _, _, ssd_dt, _, _, gdn_a, gdn_b, _, _ = _split_in(w_in)
    w_small =jnp.concatenate([_pad_lanes(ssd_dt), _pad_lanes(gdn_a[..., 0::2]), _pad_lanes(gdn_a[..., 1::2]),
                               _pad_lanes(gdn_b[..., 0::2]), _pad_lanes(gdn_b[..., 1::2])], axis=-1).astype(BF16)
    conv_w = jnp.concatenate([gdn_conv_w, ssd_conv_w], axis=-1)
    conv_b = jnp.concatenate([jnp.zeros((depth, GDN_QKV), F32), ssd_conv_b], axis=-1).reshape(depth, 1, CONV_N)
    ssd_dtb = _pad_lanes(ssd_dt_bias)
    ssd_alog = _pad_lanes(ssd_a_log)
    ssd_dskip = jnp.repeat(ssd_d, SSD_HEAD_DIM, axis=-1)
    gdn_dtb = jnp.stack([_pad_lanes(gdn_dt_bias[:, 0::2]), _pad_lanes(gdn_dt_bias[:, 1::2])], axis=1)
    gdn_alog = jnp.stack([_pad_lanes(gdn_a_log[:, 0::2]), _pad_lanes(gdn_a_log[:, 1::2])], axis=1)
    f1g, f1u, f1d = ffn1_w_gate, ffn1_w_up, ffn1_w_down
    f2g, f2u, f2d = ffn2_w_gate, ffn2_w_up, ffn2_w_down
    wo_ssd, wo_gdn, wo_attn, wo = (w_o_ssd.astype(BF16), w_o_gdn.astype(BF16), w_o_attn.astype(BF16),
                                   w_out.astype(BF16))
    bias_tiles = _bias_tiles(rel_bias, t)

    mod = _mod_all(c, w_mod, b_mod).reshape(depth, N_MOD, d)
    h = x.reshape(t, d)
    for l in range(depth):
        sh1, sc1, g1, sh2, sc2, g2, sh3, sc3, g3 = [mod[l, i][None, :] for i in range(N_MOD)]
        h = _ffn(h, norm_ffn1[l][None, :], sh1, sc1, g1, f1g, f1u, f1d, l)
        u = _modulate(h, norm_mix[l][None, :], sh2, sc2)
        conv = _matmul_conv(u, w_in_t, conv_w, conv_b, l, conv_segments)
        plain = _matmul_t(u, w_in_t, l, plain_segments)
        small = _matmul(u, w_small, l, tn=SMALL_N)
        y_ssd = _ssd(conv, plain, small, ssd_dtb[l][None, :], ssd_alog[l][None, :], ssd_dskip[l][None, :],
                     ssd_norm[l][None, :])
        y_gdn = _gdn(conv, plain, small, gdn_dtb[l], gdn_alog[l], gdn_norm[l][None, :])
        y_attn = _attn(plain, attn_q_norm[l][None, :], attn_k_norm[l][None, :], bias_tiles)
        merged = _merge(y_ssd, y_gdn, y_attn, wo_ssd, wo_gdn, wo_attn, plain, l)
        h = _out_proj(merged, wo, h, g2, l)
        h = _ffn(h, norm_ffn2[l][None, :], sh3, sc3, g3, f2g, f2u, f2d, l)
    return h.reshape(bsz, t, d)
```
